```python
import math
import jax, jax.numpy as jnp
from jax import lax
import numpy as np

D_MODEL = 1024
BATCH = 8
SEQ = 4096
DEPTH = 4
DEC_BATCH = 32
DEC_SEQ = 64
PAST_LEN = 1024

CHUNK = 64
Q_BLOCK = 128
HEAD_DIM = 64
H_FOX = 8
H_DIFF = 4
H_DSA = 8
H_IDX = 8
D_IDX = 64
TOPK_MAX = 256
ROPE_THETA = 10000.0
D_FF = ((8 * D_MODEL + 3 * 256 - 1) // (3 * 256)) * 256
W_FOX = H_FOX * HEAD_DIM
W_DIFF = H_DIFF * 2 * HEAD_DIM
W_DSA = H_DSA * HEAD_DIM
N_BRANCH = 3
SPLIT_SIZES = (W_FOX, W_FOX, W_FOX, H_FOX, W_DIFF, W_DIFF, W_DIFF, W_DSA, W_DSA, W_DSA, H_IDX * D_IDX, D_IDX, H_IDX, N_BRANCH * D_MODEL)
SPLIT_OFFSETS = tuple(sum(SPLIT_SIZES[:i + 1]) for i in range(len(SPLIT_SIZES) - 1))
N_IN = sum(SPLIT_SIZES)
LN_EPS = 1e-5
DEEPNORM_ALPHA = (2 * DEPTH) ** 0.25
DEEPNORM_BETA = (8 * DEPTH) ** -0.25
F32 = jnp.float32

kernel_name = 'hybrid_streaming_fox_diff_dsa_step'


def _layernorm(x, g, b):
    xf = x.astype(F32)
    mu = jnp.mean(xf, axis=-1, keepdims=True)
    var = jnp.mean(jnp.square(xf - mu), axis=-1, keepdims=True)
    return ((xf - mu) * lax.rsqrt(var + LN_EPS) * g.astype(F32) + b.astype(F32)).astype(x.dtype)


def _rmsnorm(x, g):
    xf = x.astype(F32)
    out = xf * lax.rsqrt(jnp.mean(jnp.square(xf), axis=-1, keepdims=True) + LN_EPS) * g.astype(F32)
    return out.astype(x.dtype)


def _rope(x, pos):
    half = x.shape[-1] // 2
    inv = ROPE_THETA ** (-jnp.arange(half, dtype=F32) / half)
    ang = pos.astype(F32)[:, None] * inv[None, :]
    shape = (1, x.shape[1]) + (1,) * (x.ndim - 3) + (half,)
    cos = jnp.cos(ang).reshape(shape)
    sin = jnp.sin(ang).reshape(shape)
    xf = x.astype(F32)
    x1, x2 = xf[..., :half], xf[..., half:]
    return jnp.concatenate([x1 * cos - x2 * sin, x2 * cos + x1 * sin], axis=-1).astype(x.dtype)


def _sweep(block_fn, n_q):
    qb = Q_BLOCK if n_q % Q_BLOCK == 0 else n_q
    out = lax.map(lambda i: block_fn(i * qb, qb), jnp.arange(n_q // qb, dtype=jnp.int32))
    out = jnp.moveaxis(out, 0, 1)
    return out.reshape((out.shape[0], n_q) + out.shape[3:])


def _fox_block(start, qb, q, k, v, cum_q, cum_k, qpos, kpos):
    qs = lax.dynamic_slice_in_dim(q, start, qb, axis=1)
    cq = lax.dynamic_slice_in_dim(cum_q, start, qb, axis=1)
    qp = lax.dynamic_slice_in_dim(qpos, start, qb, axis=0)
    s = jnp.einsum('bqhd,bkhd->bhqk', qs, k).astype(F32) * (HEAD_DIM ** -0.5)
    s = s + jnp.swapaxes(cq, 1, 2)[..., None] - jnp.swapaxes(cum_k, 1, 2)[:, :, None, :]
    mask = kpos[None, :] <= qp[:, None]
    p = jax.nn.softmax(jnp.where(mask[None, None], s, -jnp.inf), axis=-1).astype(v.dtype)
    return jnp.einsum('bhqk,bkhd->bqhd', p, v)


def _diff_block(start, qb, q, k, v, lam, qpos, kpos):
    qs = lax.dynamic_slice_in_dim(q, start, qb, axis=1)
    qp = lax.dynamic_slice_in_dim(qpos, start, qb, axis=0)
    s = jnp.einsum('bqhcd,bkhcd->bhcqk', qs, k).astype(F32) * (HEAD_DIM ** -0.5)
    mask = (kpos // CHUNK)[None, :] <= (qp // CHUNK)[:, None]
    p = jax.nn.softmax(jnp.where(mask[None, None, None], s, -jnp.inf), axis=-1)
    a = (p[:, :, 0] - lam * p[:, :, 1]).astype(v.dtype)
    return jnp.einsum('bhqk,bkhe->bqhe', a, v)


def _dsa_block(start, qb, q, k, v, qi, ki, wi, qpos, kpos, k_sel):
    qs = lax.dynamic_slice_in_dim(q, start, qb, axis=1)
    qis = lax.dynamic_slice_in_dim(qi, start, qb, axis=1)
    wis = lax.dynamic_slice_in_dim(wi, start, qb, axis=1)
    qp = lax.dynamic_slice_in_dim(qpos, start, qb, axis=0)
    idx = jnp.einsum('bqhd,bkd->bqhk', qis, ki).astype(F32) * (D_IDX ** -0.5)
    score = jnp.einsum('bqhk,bqh->bqk', jax.nn.relu(idx), wis.astype(F32)) * (H_IDX ** -0.5)
    adm = (kpos // CHUNK)[None, :] <= (qp // CHUNK)[:, None]
    score = jnp.where(adm[None], score, -jnp.inf)
    _, top_idx = lax.top_k(score, k_sel)
    valid = (kpos[top_idx] // CHUNK) <= (qp // CHUNK)[None, :, None]
    gather = jax.vmap(lambda rows, ids: rows[ids])
    k_g = gather(k, top_idx)
    v_g = gather(v, top_idx)
    s = jnp.einsum('bqhd,bqkhd->bhqk', qs, k_g).astype(F32) * (HEAD_DIM ** -0.5)
    p = jax.nn.softmax(jnp.where(valid[:, None], s, -jnp.inf), axis=-1).astype(v.dtype)
    return jnp.einsum('bhqk,bqkhd->bqhd', p, v_g)


def _empty_past(bsz, dtype):
    z = lambda *s: jnp.zeros((bsz, 0) + s, dtype)
    return (z(H_FOX, HEAD_DIM), z(H_FOX, HEAD_DIM), z(H_FOX), z(H_DIFF, 2, HEAD_DIM), z(H_DIFF, 2 * HEAD_DIM), z(H_DSA, HEAD_DIM), z(H_DSA, HEAD_DIM), z(D_IDX))


def _layer(x, qpos, past, lp, layer_idx):
    pa_k, pa_v, pa_f, pb_k, pb_v, pc_k, pc_v, pc_i = past
    (w_in, b_fgate, b_gate, lam_q1, lam_k1, lam_q2, lam_k2, diff_g, w_br_a, w_br_b, w_br_c,
     w_o, ln1_g, ln1_b, ln2_g, ln2_b, w_ff1, w_ff3, w_ff2) = lp
    bsz, n_new, _ = x.shape
    n_keys = pa_k.shape[1] + n_new
    kpos = jnp.arange(n_keys, dtype=jnp.int32)
    (aq, ak, av, af, bq, bk, bv, cq, ck, cv, iq, ik, iw, gl) = jnp.split(x @ w_in, SPLIT_OFFSETS, axis=-1)

    aq = aq.reshape(bsz, n_new, H_FOX, HEAD_DIM)
    ak = ak.reshape(bsz, n_new, H_FOX, HEAD_DIM)
    av = av.reshape(bsz, n_new, H_FOX, HEAD_DIM)
    logf = jax.nn.log_sigmoid(af.astype(F32) + b_fgate.astype(F32))
    a_keys = jnp.concatenate([pa_k, ak], axis=1)
    a_vals = jnp.concatenate([pa_v, av], axis=1)
    cum = jnp.cumsum(jnp.concatenate([pa_f.astype(F32), logf], axis=1), axis=1)
    cum_q = cum[:, n_keys - n_new:]
    ya = _sweep(lambda s, n: _fox_block(s, n, aq, a_keys, a_vals, cum_q, cum, qpos, kpos), n_new)

    bq = _rope(bq.reshape(bsz, n_new, H_DIFF, 2, HEAD_DIM), qpos)
    bk = _rope(bk.reshape(bsz, n_new, H_DIFF, 2, HEAD_DIM), qpos)
    bv = bv.reshape(bsz, n_new, H_DIFF, 2 * HEAD_DIM)
    b_keys = jnp.concatenate([pb_k, bk], axis=1)
    b_vals = jnp.concatenate([pb_v, bv], axis=1)
    lam_init = 0.8 - 0.6 * math.exp(-0.3 * layer_idx)
    lam = (jnp.exp(jnp.sum(lam_q1.astype(F32) * lam_k1.astype(F32)))
           - jnp.exp(jnp.sum(lam_q2.astype(F32) * lam_k2.astype(F32))) + lam_init)
    yb = _sweep(lambda s, n: _diff_block(s, n, bq, b_keys, b_vals, lam, qpos, kpos), n_new)
    yb = _rmsnorm(yb, diff_g) * (1.0 - lam_init)

    cq = _rope(cq.reshape(bsz, n_new, H_DSA, HEAD_DIM), qpos)
    ck = _rope(ck.reshape(bsz, n_new, H_DSA, HEAD_DIM), qpos)
    cv = cv.reshape(bsz, n_new, H_DSA, HEAD_DIM)
    iq = _rope(iq.reshape(bsz, n_new, H_IDX, D_IDX), qpos)
    ik = _rope(ik, qpos)
    c_keys = jnp.concatenate([pc_k, ck], axis=1)
    c_vals = jnp.concatenate([pc_v, cv], axis=1)
    i_keys = jnp.concatenate([pc_i, ik], axis=1)
    k_sel = min(TOPK_MAX, n_keys // 4)
    yc = _sweep(lambda s, n: _dsa_block(s, n, cq, c_keys, c_vals, iq, i_keys, iw, qpos, kpos, k_sel), n_new)

    gates = jax.nn.sigmoid(gl.reshape(bsz, n_new, N_BRANCH, D_MODEL).astype(F32) + b_gate.astype(F32)).astype(x.dtype)
    merged = (gates[:, :, 0] * (ya.reshape(bsz, n_new, W_FOX) @ w_br_a)
              + gates[:, :, 1] * (yb.reshape(bsz, n_new, W_DIFF) @ w_br_b)
              + gates[:, :, 2] * (yc.reshape(bsz, n_new, W_DSA) @ w_br_c))
    x = _layernorm(DEEPNORM_ALPHA * x + merged @ w_o, ln1_g, ln1_b)
    ff = (jax.nn.silu(x @ w_ff1) * (x @ w_ff3)) @ w_ff2
    x = _layernorm(DEEPNORM_ALPHA * x + ff, ln2_g, ln2_b)
    return x, (ak, av, logf.astype(x.dtype), bk, bv, ck, cv, ik)


def setup_inputs(seed: int = 0) -> dict:
    key = jax.random.key(seed)
    ks = jax.random.split(key, 32)
    nrm = lambda k, shape, scale=1.0: jax.random.normal(k, shape, F32) * scale
    L = (DEPTH,)
    cb = (DEPTH, DEC_BATCH, PAST_LEN)
    return {
        'x_prompt': nrm(ks[0], (BATCH, SEQ, D_MODEL)),
        'x_sample': nrm(ks[1], (DEC_BATCH, DEC_SEQ, D_MODEL)),
        'cache_a_k': nrm(ks[2], cb + (H_FOX, HEAD_DIM)),
        'cache_a_v': nrm(ks[3], cb + (H_FOX, HEAD_DIM)),
        'cache_a_logf': jax.nn.log_sigmoid(2.0 + nrm(ks[4], cb + (H_FOX,))),
        'cache_b_k': nrm(ks[5], cb + (H_DIFF, 2, HEAD_DIM)),
        'cache_b_v': nrm(ks[6], cb + (H_DIFF, 2 * HEAD_DIM)),
        'cache_c_k': nrm(ks[7], cb + (H_DSA, HEAD_DIM)),
        'cache_c_v': nrm(ks[8], cb + (H_DSA, HEAD_DIM)),
        'cache_c_idx': nrm(ks[9], cb + (D_IDX,)),
        'w_in': nrm(ks[10], L + (D_MODEL, N_IN), D_MODEL ** -0.5),
        'b_fgate': 2.0 + nrm(ks[11], L + (H_FOX,), 0.1),
        'b_gate': nrm(ks[12], L + (N_BRANCH, D_MODEL), 0.01),
        'lam_q1': nrm(ks[13], L + (HEAD_DIM,), 0.1),
        'lam_k1': nrm(ks[14], L + (HEAD_DIM,), 0.1),
        'lam_q2': nrm(ks[15], L + (HEAD_DIM,), 0.1),
        'lam_k2': nrm(ks[16], L + (HEAD_DIM,), 0.1),
        'diff_norm_g': 1.0 + nrm(ks[17], L + (2 * HEAD_DIM,), 0.01),
        'w_br_a': nrm(ks[18], L + (W_FOX, D_MODEL), W_FOX ** -0.5),
        'w_br_b': nrm(ks[19], L + (W_DIFF, D_MODEL), W_DIFF ** -0.5),
        'w_br_c': nrm(ks[20], L + (W_DSA, D_MODEL), W_DSA ** -0.5),
        'w_o': nrm(ks[21], L + (D_MODEL, D_MODEL), D_MODEL ** -0.5 * DEEPNORM_BETA),
        'ln1_g': 1.0 + nrm(ks[22], L + (D_MODEL,), 0.01),
        'ln1_b': nrm(ks[23], L + (D_MODEL,), 0.01),
        'ln2_g': 1.0 + nrm(ks[24], L + (D_MODEL,), 0.01),
        'ln2_b': nrm(ks[25], L + (D_MODEL,), 0.01),
        'w_ff1': nrm(ks[26], L + (D_MODEL, D_FF), D_MODEL ** -0.5),
        'w_ff3': nrm(ks[27], L + (D_MODEL, D_FF), D_MODEL ** -0.5),
        'w_ff2': nrm(ks[28], L + (D_FF, D_MODEL), D_FF ** -0.5 * DEEPNORM_BETA),
    }


def reference(x_prompt, x_sample, cache_a_k, cache_a_v, cache_a_logf, cache_b_k, cache_b_v,
              cache_c_k, cache_c_v, cache_c_idx, w_in, b_fgate, b_gate, lam_q1, lam_k1, lam_q2,
              lam_k2, diff_norm_g, w_br_a, w_br_b, w_br_c, w_o, ln1_g, ln1_b, ln2_g, ln2_b,
              w_ff1, w_ff3, w_ff2):
    params = (w_in, b_fgate, b_gate, lam_q1, lam_k1, lam_q2, lam_k2, diff_norm_g, w_br_a, w_br_b,
              w_br_c, w_o, ln1_g, ln1_b, ln2_g, ln2_b, w_ff1, w_ff3, w_ff2)
    caches = (cache_a_k, cache_a_v, cache_a_logf, cache_b_k, cache_b_v, cache_c_k, cache_c_v, cache_c_idx)
    past_len = cache_a_k.shape[2]
    qpos_p = jnp.arange(x_prompt.shape[1], dtype=jnp.int32)
    qpos_s = past_len + jnp.arange(x_sample.shape[1], dtype=jnp.int32)
    yp, ys = x_prompt, x_sample
    rows_p, rows_s = [], []
    for l in range(DEPTH):
        lp = tuple(w[l] for w in params)
        yp, rp = _layer(yp, qpos_p, _empty_past(yp.shape[0], yp.dtype), lp, l)
        ys, rs = _layer(ys, qpos_s, tuple(c[l] for c in caches), lp, l)
        rows_p.append(rp)
        rows_s.append(rs)
    p_a_k, p_a_v, p_a_logf, p_b_k, p_b_v, p_c_k, p_c_v, p_c_idx = [jnp.stack([r[i] for r in rows_p]) for i in range(8)]
    s_a_k, s_a_v, s_a_logf, s_b_k, s_b_v, s_c_k, s_c_v, s_c_idx = [jnp.stack([r[i] for r in rows_s]) for i in range(8)]
    return (yp, ys, p_a_k, p_a_v, p_a_logf, p_b_k, p_b_v, p_c_k, p_c_v, p_c_idx,
            s_a_k, s_a_v, s_a_logf, s_b_k, s_b_v, s_c_k, s_c_v, s_c_idx)
```

```python
import functools
import math

import jax
import jax.numpy as jnp
import numpy as np
from jax import lax
from jax.experimental import pallas as pl
from jax.experimental.pallas import tpu as pltpu

F32 = jnp.float32
BF16 = jnp.bfloat16
I32 = jnp.int32

D_MODEL = 1024
HEAD_DIM = 64
H_FOX = 8
H_DIFF = 4
H_DSA = 8
H_IDX = 8
D_IDX = 64
CHUNK = 64
CHUNK_SHIFT = 6
TOPK_MAX = 256
ROPE_THETA = 10000.0
N_BRANCH = 3
LN_EPS = 1e-5
MODEL_DEPTH = 4
DEEPNORM_ALPHA = (2 * MODEL_DEPTH) ** 0.25
W_HEADS = 512
SPLIT_SIZES = (W_HEADS, W_HEADS, W_HEADS, H_FOX, W_HEADS, W_HEADS, W_HEADS, W_HEADS, W_HEADS, W_HEADS,
               H_IDX * D_IDX, D_IDX, H_IDX, N_BRANCH * D_MODEL)

LANES = 128
N_GROUPS = 11
SLAB_W = N_GROUPS * W_HEADS
G_AQ, G_AK, G_AV, G_BQ, G_BK, G_BV, G_CQ, G_CK, G_CV, G_IQ, G_MISC = range(N_GROUPS)
MISC_LOGF = 64
MISC_IW = 72
VMEM_LIMIT = 56 * 1024 * 1024
NEG_INF = float("-inf")
INT_MIN = np.int32(-2 ** 31)
INT_MAX_MASK = np.int32(2 ** 31 - 1)

TM_DENSE = 512
TQ_ATTN = 512
TQ_DSA = 256
SEG_DSA = 512
CH_DSA = 512


def _cparams(n_axes):
    return pltpu.CompilerParams(dimension_semantics=("arbitrary",) * n_axes, vmem_limit_bytes=VMEM_LIMIT)


def _lane_iota(n=LANES):
    return lax.broadcasted_iota(I32, (1, n), 1)


def _f32_group(j):
    return jnp.minimum(2 * (j // 3) + (j % 3) // 2, 5)


def _inproj_kernel(x_ref, w_ref, cos_ref, sin_ref, bf_ref, s16_ref, s32_ref, misc_ref, xb_ref):
    j = pl.program_id(1)

    @pl.when(j == 0)
    def _():
        xb_ref[...] = x_ref[...].astype(BF16)

    h = jnp.dot(xb_ref[...], w_ref[...], preferred_element_type=F32)
    lane = _lane_iota()
    sel_up = (lane & (HEAD_DIM - 1)) < HEAD_DIM // 2

    def rope128(xb):
        up = pltpu.roll(xb, LANES - HEAD_DIM // 2, 1)
        dn = pltpu.roll(xb, HEAD_DIM // 2, 1)
        return xb * cos_ref[...] + jnp.where(sel_up, up, dn) * sin_ref[...]

    is_rope = (j == G_BQ) | (j == G_BK) | (j == G_CQ) | (j == G_CK) | (j == G_IQ)
    has32 = (j == G_AK) | (j == G_AV) | (j == G_BK) | (j == G_BV) | (j == G_CK) | (j == G_CV)

    def write(val_fn):
        for c in range(W_HEADS // LANES):
            sl = slice(c * LANES, (c + 1) * LANES)
            v = val_fn(h[:, sl])
            s16_ref[:, sl] = v.astype(BF16)

            @pl.when(has32)
            def _():
                s32_ref[:, sl] = v

    @pl.when(is_rope)
    def _():
        write(rope128)

    @pl.when(jnp.logical_not(is_rope) & (j < G_MISC))
    def _():
        write(lambda v: v)

    @pl.when(j == G_MISC)
    def _():
        hb = h[:, :LANES]
        roped = rope128(hb)
        z = hb + bf_ref[...]
        logf = jnp.minimum(z, 0.0) - jnp.log1p(jnp.exp(-jnp.abs(z)))
        misc_ref[...] = jnp.where(lane < MISC_LOGF, roped, jnp.where(lane < MISC_IW, logf, hb))
        ikd = jnp.where(lane < D_IDX, roped, pltpu.roll(roped, D_IDX, 1))
        s16_ref[:, :LANES] = ikd.astype(BF16)
        s16_ref[:, LANES:] = jnp.zeros((s16_ref.shape[0], W_HEADS - LANES), BF16)


def _inproj(x, w_all, layer, cos_t, sin_t, bf_row, seq):
    m = x.shape[0]
    tm = min(TM_DENSE, m)
    n_tab = cos_t.shape[0] // tm
    tab_map = lambda i, j: (i % n_tab, 0)
    return pl.pallas_call(
        _inproj_kernel,
        grid=(m // tm, N_GROUPS),
        in_specs=[
            pl.BlockSpec((tm, D_MODEL), lambda i, j: (i, 0)),
            pl.BlockSpec((None, D_MODEL, W_HEADS), lambda i, j: (layer, 0, j)),
            pl.BlockSpec((tm, LANES), tab_map),
            pl.BlockSpec((tm, LANES), tab_map),
            pl.BlockSpec((None, 1, LANES), lambda i, j: (layer, 0, 0)),
        ],
        out_specs=[
            pl.BlockSpec((tm, W_HEADS), lambda i, j: (i, j)),
            pl.BlockSpec((tm, W_HEADS), lambda i, j: (i, _f32_group(j))),
            pl.BlockSpec((tm, LANES), lambda i, j: (i, 0)),
        ],
        out_shape=[
            jax.ShapeDtypeStruct((m, SLAB_W), BF16),
            jax.ShapeDtypeStruct((m, 6 * W_HEADS), F32),
            jax.ShapeDtypeStruct((m, LANES), F32),
        ],
        scratch_shapes=[pltpu.VMEM((tm, D_MODEL), BF16)],
        compiler_params=_cparams(2),
        name="inproj",
    )(x, w_all, cos_t, sin_t, bf_row)


def _split3(x):
    hi = x.astype(BF16)
    r1 = x - hi.astype(F32)
    mid = r1.astype(BF16)
    lo = (r1 - mid.astype(F32)).astype(BF16)
    return hi, mid, lo


def _cumsum_kernel(x_ref, o_ref, carry_ref):
    t = pl.program_id(1)

    @pl.when(t == 0)
    def _():
        carry_ref[...] = jnp.zeros_like(carry_ref)

    x = x_ref[0]
    tl = x.shape[0]
    r = lax.broadcasted_iota(I32, (tl, tl), 0)
    c = lax.broadcasted_iota(I32, (tl, tl), 1)
    tri = jnp.where(c <= r, 1.0, 0.0).astype(BF16)
    hi, mid, lo = _split3(x)
    dot = lambda a: jnp.dot(tri, a, preferred_element_type=F32)
    cs = (dot(lo) + dot(mid)) + dot(hi) + carry_ref[...]
    o_ref[0] = cs
    carry_ref[...] = cs[tl - 1:tl, :]


def _cumsum_time(x):
    b, l, h = x.shape
    tl = 512 if l % 512 == 0 else l
    return pl.pallas_call(
        _cumsum_kernel,
        grid=(b, l // tl),
        in_specs=[pl.BlockSpec((1, tl, h), lambda i, t: (i, t, 0))],
        out_specs=pl.BlockSpec((1, tl, h), lambda i, t: (i, t, 0)),
        out_shape=jax.ShapeDtypeStruct((b, l, h), F32),
        scratch_shapes=[pltpu.VMEM((1, h), F32)],
        compiler_params=_cparams(2),
        name="cumsum",
    )(x)


def _split_pair(q):
    lane = _lane_iota()
    zero = jnp.zeros_like(q)
    return jnp.where(lane < HEAD_DIM, q, zero), jnp.where(lane >= HEAD_DIM, q, zero)


def _qk(q, k):
    return lax.dot_general(q, k, (((1,), (1,)), ((), ())), preferred_element_type=F32)


def _online_update(s, v, m_old, l_old, acc_old):
    m_new = jnp.maximum(m_old, jnp.max(s, axis=1, keepdims=True))
    m_safe = jnp.where(m_new == NEG_INF, 0.0, m_new)
    alpha = jnp.exp(m_old - m_safe)
    p = jnp.exp(s - m_safe)
    l_new = alpha * l_old + jnp.sum(p, axis=1, keepdims=True)
    acc_new = alpha * acc_old + jnp.dot(p.astype(BF16), v, preferred_element_type=F32)
    return m_new, l_new, acc_new


def _causal_mask(kind, rows, cols, row0, col0):
    r = row0 + lax.broadcasted_iota(I32, (rows, 1), 0)
    c = col0 + lax.broadcasted_iota(I32, (1, cols), 1)
    if kind == "fox":
        return c <= r
    return (c >> CHUNK_SHIFT) <= (r >> CHUNK_SHIFT)


def _diff_lambda(lam_ref, lam_init):
    lp = lam_ref[...]
    s1 = jnp.sum(lp[0:1] * lp[1:2], axis=1, keepdims=True)
    s2 = jnp.sum(lp[2:3] * lp[3:4], axis=1, keepdims=True)
    return jnp.exp(s1) - jnp.exp(s2) + lam_init


def _pair_epilogue(kind, o0, o1, lam_ref, g_ref, lam_init):
    if kind == "fox":
        return jnp.where(_lane_iota() < HEAD_DIM, o0, o1)
    o = o0 - _diff_lambda(lam_ref, lam_init) * o1
    o = o * lax.rsqrt(jnp.mean(jnp.square(o), axis=1, keepdims=True) + LN_EPS) * g_ref[...]
    return o * (1.0 - lam_init)


def _pair_attn_prompt_kernel(*refs, kind, tq, lam_init):
    if kind == "fox":
        q_ref, k_ref, v_ref, cq_ref, ck_ref, o_ref, m_ref, l_ref, acc_ref = refs
        lam_ref = g_ref = None
    else:
        q_ref, k_ref, v_ref, lam_ref, g_ref, o_ref, m_ref, l_ref, acc_ref = refs
    qi = pl.program_id(2)
    qs = _split_pair(q_ref[...])
    m_ref[...] = jnp.full(m_ref.shape, NEG_INF, F32)
    l_ref[...] = jnp.zeros(l_ref.shape, F32)
    acc_ref[...] = jnp.zeros(acc_ref.shape, F32)

    def step(j, masked):
        start = pl.multiple_of(j * tq, tq)
        k = k_ref[0, pl.ds(start, tq), :]
        v = v_ref[0, pl.ds(start, tq), :]
        for h in range(2):
            s = _qk(qs[h], k)
            if kind == "fox":
                s = s + cq_ref[0, 0, :, h:h + 1] - ck_ref[0, 0, h:h + 1, pl.ds(start, tq)]
            if masked:
                s = jnp.where(_causal_mask(kind, tq, tq, 0, 0), s, NEG_INF)
            m_ref[h], l_ref[h], acc_ref[h] = _online_update(s, v, m_ref[h], l_ref[h], acc_ref[h])

    def body(j, carry):
        step(j, False)
        return carry

    lax.fori_loop(0, qi, body, 0)
    step(qi, True)
    o0 = acc_ref[0] * (1.0 / l_ref[0])
    o1 = acc_ref[1] * (1.0 / l_ref[1])
    o_ref[...] = _pair_epilogue(kind, o0, o1, lam_ref, g_ref, lam_init).astype(BF16)


def _pair_attn_prompt(kind, slab2, slab3, qg, kg, vg, extra, lam_init):
    b, t, _ = slab3.shape
    tq = min(TQ_ATTN, t)
    nq = t // tq
    npair = W_HEADS // LANES
    in_specs = [
        pl.BlockSpec((tq, LANES), lambda i, p, q: (i * nq + q, qg * npair + p)),
        pl.BlockSpec((1, t, LANES), lambda i, p, q: (i, 0, kg * npair + p)),
        pl.BlockSpec((1, t, LANES), lambda i, p, q: (i, 0, vg * npair + p)),
    ]
    if kind == "fox":
        cq4, ck4 = extra
        in_specs += [
            pl.BlockSpec((1, 1, tq, 2), lambda i, p, q: (i, p, q, 0)),
            pl.BlockSpec((1, 1, 2, t), lambda i, p, q: (i, p, 0, 0)),
        ]
    else:
        lam4, g_row = extra
        in_specs += [
            pl.BlockSpec((4, HEAD_DIM), lambda i, p, q: (0, 0)),
            pl.BlockSpec((1, LANES), lambda i, p, q: (0, 0)),
        ]
    return pl.pallas_call(
        functools.partial(_pair_attn_prompt_kernel, kind=kind, tq=tq, lam_init=lam_init),
        grid=(b, npair, nq),
        in_specs=in_specs,
        out_specs=pl.BlockSpec((tq, LANES), lambda i, p, q: (i * nq + q, p)),
        out_shape=jax.ShapeDtypeStruct((b * t, W_HEADS), BF16),
        scratch_shapes=[pltpu.VMEM((2, tq, 1), F32), pltpu.VMEM((2, tq, 1), F32), pltpu.VMEM((2, tq, LANES), F32)],
        compiler_params=_cparams(3),
        name=kind + "_prompt",
    )(slab2, slab3, slab3, *extra)


def _pair_attn_sample_kernel(*refs, kind, t, past, lam_init):
    if kind == "fox":
        q_ref, kp_ref, vp_ref, kn_ref, vn_ref, cq_ref, ck_ref, o_ref = refs
        lam_ref = g_ref = None
    else:
        q_ref, kp_ref, vp_ref, kn_ref, vn_ref, lam_ref, g_ref, o_ref = refs
    qs = _split_pair(q_ref[...])
    kp = kp_ref[...].astype(BF16)
    vp = vp_ref[...].astype(BF16)
    kn = kn_ref[...]
    vn = vn_ref[...]
    outs = []
    for h in range(2):
        m = jnp.full((t, 1), NEG_INF, F32)
        l = jnp.zeros((t, 1), F32)
        acc = jnp.zeros((t, LANES), F32)
        s = _qk(qs[h], kp)
        if kind == "fox":
            s = s + cq_ref[0, 0, :, h:h + 1] - ck_ref[0, 0, h:h + 1, :past]
        m, l, acc = _online_update(s, vp, m, l, acc)
        s = _qk(qs[h], kn)
        if kind == "fox":
            s = s + cq_ref[0, 0, :, h:h + 1] - ck_ref[0, 0, h:h + 1, past:past + t]
        s = jnp.where(_causal_mask(kind, t, t, past, past), s, NEG_INF)
        m, l, acc = _online_update(s, vn, m, l, acc)
        outs.append(acc * (1.0 / l))
    o_ref[...] = _pair_epilogue(kind, outs[0], outs[1], lam_ref, g_ref, lam_init).astype(BF16)


def _pair_attn_sample(kind, slab2, t, cache_k, cache_v, layer, qg, kg, vg, extra, lam_init):
    b = slab2.shape[0] // t
    past = cache_k.shape[2]
    npair = W_HEADS // LANES
    in_specs = [
        pl.BlockSpec((t, LANES), lambda i, p: (i, qg * npair + p)),
        pl.BlockSpec((None, None, past, LANES), lambda i, p: (layer, i, 0, p)),
        pl.BlockSpec((None, None, past, LANES), lambda i, p: (layer, i, 0, p)),
        pl.BlockSpec((t, LANES), lambda i, p: (i, kg * npair + p)),
        pl.BlockSpec((t, LANES), lambda i, p: (i, vg * npair + p)),
    ]
    if kind == "fox":
        in_specs += [
            pl.BlockSpec((1, 1, t, 2), lambda i, p: (i, p, 0, 0)),
            pl.BlockSpec((1, 1, 2, past + t), lambda i, p: (i, p, 0, 0)),
        ]
    else:
        in_specs += [
            pl.BlockSpec((4, HEAD_DIM), lambda i, p: (0, 0)),
            pl.BlockSpec((1, LANES), lambda i, p: (0, 0)),
        ]
    return pl.pallas_call(
        functools.partial(_pair_attn_sample_kernel, kind=kind, t=t, past=past, lam_init=lam_init),
        grid=(b, npair),
        in_specs=in_specs,
        out_specs=pl.BlockSpec((t, LANES), lambda i, p: (i, p)),
        out_shape=jax.ShapeDtypeStruct((b * t, W_HEADS), BF16),
        compiler_params=_cparams(2),
        name=kind + "_sample",
    )(slab2, cache_k, cache_v, slab2, slab2, *extra)


def _dsa_core(iq_ref, iw, ikd_ref, cq_ref, ck_ref, cv_ref, o_ref, keys_ref, bias_ref, m_ref, l_ref, acc_ref,
              *, tq, n_l, ch, qpos0, n_keys, k_sel):
    nch = n_l // ch
    lane = _lane_iota()
    qchunk = (qpos0 + lax.broadcasted_iota(I32, (tq, 1), 0)) >> CHUNK_SHIFT
    w = [iw[:, MISC_IW + h:MISC_IW + h + 1] for h in range(H_IDX)]

    def score_chunk(c, carry):
        start = pl.multiple_of(c * ch, ch)
        ik = ikd_ref[pl.ds(start, ch), :]
        acc = jnp.zeros((tq, ch), F32)
        for hp in range(H_IDX // 2):
            pair = _split_pair(iq_ref[:, hp * LANES:(hp + 1) * LANES])
            for half in range(2):
                acc = acc + jnp.maximum(_qk(pair[half], ik), 0.0) * w[2 * hp + half]
        score = acc * (H_IDX ** -0.5) + 0.0
        bits = lax.bitcast_convert_type(score, I32)
        key = bits ^ ((bits >> 31) & INT_MAX_MASK)
        kpos = start + lax.broadcasted_iota(I32, (1, ch), 1)
        adm = ((kpos >> CHUNK_SHIFT) <= qchunk) & (kpos < n_keys)
        keys_ref[:, pl.ds(start, ch)] = jnp.where(adm, key, INT_MIN)
        return carry

    lax.fori_loop(0, nch, score_chunk, 0)

    def count(pred):
        def body(c, a):
            start = pl.multiple_of(c * ch, ch)
            hit = jnp.where(pred(keys_ref[:, pl.ds(start, ch)]), 1.0, 0.0)
            for t in range(ch // LANES):
                a = a + hit[:, t * LANES:(t + 1) * LANES]
            return a
        part = lax.fori_loop(0, nch, body, jnp.zeros((tq, LANES), F32))
        return jnp.sum(part, axis=1, keepdims=True)

    def bit_step(i, thr_u):
        cand_u = thr_u | lax.shift_left(jnp.int32(1), 31 - i)
        cand_s = cand_u ^ INT_MIN
        n_ge = count(lambda kk: kk >= cand_s)
        return jnp.where(n_ge >= k_sel, cand_u, thr_u)

    thr = lax.fori_loop(0, 32, bit_step, jnp.zeros((tq, 1), I32)) ^ INT_MIN
    need = k_sel - count(lambda kk: kk > thr)

    r128 = lax.broadcasted_iota(I32, (LANES, LANES), 0)
    c128 = lax.broadcasted_iota(I32, (LANES, LANES), 1)
    before = jnp.where(r128 < c128, 1.0, 0.0).astype(BF16)

    def select_block(t, seen):
        start = pl.multiple_of(t * LANES, LANES)
        kk = keys_ref[:, pl.ds(start, LANES)]
        eq = kk == thr
        eqf = jnp.where(eq, 1.0, 0.0)
        rank = seen + jnp.dot(eqf.astype(BF16), before, preferred_element_type=F32)
        sel = ((kk > thr) | (eq & (rank < need))) & (kk != INT_MIN)
        bias_ref[:, pl.ds(start, LANES)] = jnp.where(sel, 0.0, NEG_INF)
        return seen + jnp.sum(eqf, axis=1, keepdims=True)

    lax.fori_loop(0, n_l // LANES, select_block, jnp.zeros((tq, 1), F32))

    for p in range(H_DSA // 2):
        sl = slice(p * LANES, (p + 1) * LANES)
        qs = _split_pair(cq_ref[:, sl])
        m_ref[...] = jnp.full(m_ref.shape, NEG_INF, F32)
        l_ref[...] = jnp.zeros(l_ref.shape, F32)
        acc_ref[...] = jnp.zeros(acc_ref.shape, F32)

        def attend(c, carry):
            start = pl.multiple_of(c * ch, ch)
            k = ck_ref[pl.ds(start, ch), sl]
            v = cv_ref[pl.ds(start, ch), sl]
            bias = bias_ref[:, pl.ds(start, ch)]
            for h in range(2):
                s = _qk(qs[h], k) + bias
                m_ref[h], l_ref[h], acc_ref[h] = _online_update(s, v, m_ref[h], l_ref[h], acc_ref[h])
            return carry

        lax.fori_loop(0, nch, attend, 0)
        o0 = acc_ref[0] * (1.0 / l_ref[0])
        o1 = acc_ref[1] * (1.0 / l_ref[1])
        o_ref[:, sl] = jnp.where(lane < HEAD_DIM, o0, o1).astype(BF16)


def _dsa_scratch(tq, n_l):
    return [pltpu.VMEM((tq, n_l), I32), pltpu.VMEM((tq, n_l), F32),
            pltpu.VMEM((2, tq, 1), F32), pltpu.VMEM((2, tq, 1), F32), pltpu.VMEM((2, tq, LANES), F32)]


def _dsa_prompt_kernel(iq_ref, misc_ref, ikd_ref, cq_ref, ck_ref, cv_ref, o_ref, *scratch, tq, n_l, q0, n_keys, k_sel):
    qpos0 = q0 + pl.program_id(1) * tq
    _dsa_core(iq_ref, misc_ref[...], ikd_ref.at[0], cq_ref, ck_ref.at[0], cv_ref.at[0], o_ref, *scratch,
              tq=tq, n_l=n_l, ch=min(CH_DSA, n_l), qpos0=qpos0, n_keys=n_keys, k_sel=k_sel)


def _dsa_prompt_segment(slab2, slab3, misc, seg, seg_len, k_sel):
    b, t, _ = slab3.shape
    tq = min(TQ_DSA, seg_len)
    nqs = seg_len // tq
    nq = t // tq
    n_l = (seg + 1) * seg_len
    row = lambda i, q: i * nq + seg * nqs + q
    npair = W_HEADS // LANES
    return pl.pallas_call(
        functools.partial(_dsa_prompt_kernel, tq=tq, n_l=n_l, q0=seg * seg_len, n_keys=t, k_sel=k_sel),
        grid=(b, nqs),
        in_specs=[
            pl.BlockSpec((tq, W_HEADS), lambda i, q: (row(i, q), G_IQ)),
            pl.BlockSpec((tq, LANES), lambda i, q: (row(i, q), 0)),
            pl.BlockSpec((1, n_l, LANES), lambda i, q: (i, 0, G_MISC * npair)),
            pl.BlockSpec((tq, W_HEADS), lambda i, q: (row(i, q), G_CQ)),
            pl.BlockSpec((1, n_l, W_HEADS), lambda i, q: (i, 0, G_CK)),
            pl.BlockSpec((1, n_l, W_HEADS), lambda i, q: (i, 0, G_CV)),
        ],
        out_specs=pl.BlockSpec((tq, W_HEADS), lambda i, q: (i * nqs + q, 0)),
        out_shape=jax.ShapeDtypeStruct((b * seg_len, W_HEADS), BF16),
        scratch_shapes=_dsa_scratch(tq, n_l),
        compiler_params=_cparams(2),
        name="dsa_prompt",
    )(slab2, misc, slab3, slab2, slab3, slab3)


def _dsa_prompt(slab2, slab3, misc, k_sel):
    b, t, _ = slab3.shape
    seg_len = min(SEG_DSA, t)
    segs = [_dsa_prompt_segment(slab2, slab3, misc, s, seg_len, k_sel) for s in range(t // seg_len)]
    y = jnp.stack([s.reshape(b, seg_len, W_HEADS) for s in segs], axis=1)
    return y.reshape(b * t, W_HEADS)


def _dsa_sample_kernel(iq_ref, misc_ref, ikp_ref, ckp_ref, cvp_ref, ikn_ref, cq_ref, ckn_ref, cvn_ref, o_ref,
                       ik_s, ck_s, cv_s, *scratch, t, past, n_l, ch, k_sel):
    pad = n_l - past - t
    ik_s[:past, :] = ikp_ref[...]
    ik_s[past:past + t, :] = ikn_ref[...]
    ik_s[past + t:, :] = jnp.zeros((pad, LANES), BF16)
    ck_s[:past, :] = ckp_ref[...].astype(BF16)
    ck_s[past:past + t, :] = ckn_ref[...]
    ck_s[past + t:, :] = jnp.zeros((pad, W_HEADS), BF16)
    cv_s[:past, :] = cvp_ref[...].astype(BF16)
    cv_s[past:past + t, :] = cvn_ref[...]
    cv_s[past + t:, :] = jnp.zeros((pad, W_HEADS), BF16)
    _dsa_core(iq_ref, misc_ref[...], ik_s, cq_ref, ck_s, cv_s, o_ref, *scratch,
              tq=t, n_l=n_l, ch=ch, qpos0=past, n_keys=past + t, k_sel=k_sel)


def _dsa_sample(slab2, misc, t, ikd_past, cache_k, cache_v, layer, k_sel):
    b = slab2.shape[0] // t
    past = cache_k.shape[2]
    ch = 3 * LANES
    n_l = -(-(past + t) // ch) * ch
    npair = W_HEADS // LANES
    return pl.pallas_call(
        functools.partial(_dsa_sample_kernel, t=t, past=past, n_l=n_l, ch=ch, k_sel=k_sel),
        grid=(b,),
        in_specs=[
            pl.BlockSpec((t, W_HEADS), lambda i: (i, G_IQ)),
            pl.BlockSpec((t, LANES), lambda i: (i, 0)),
            pl.BlockSpec((None, None, past, LANES), lambda i: (layer, i, 0, 0)),
            pl.BlockSpec((None, None, past, W_HEADS), lambda i: (layer, i, 0, 0)),
            pl.BlockSpec((None, None, past, W_HEADS), lambda i: (layer, i, 0, 0)),
            pl.BlockSpec((t, LANES), lambda i: (i, G_MISC * npair)),
            pl.BlockSpec((t, W_HEADS), lambda i: (i, G_CQ)),
            pl.BlockSpec((t, W_HEADS), lambda i: (i, G_CK)),
            pl.BlockSpec((t, W_HEADS), lambda i: (i, G_CV)),
        ],
        out_specs=pl.BlockSpec((t, W_HEADS), lambda i: (i, 0)),
        out_shape=jax.ShapeDtypeStruct((b * t, W_HEADS), BF16),
        scratch_shapes=[pltpu.VMEM((n_l, LANES), BF16), pltpu.VMEM((n_l, W_HEADS), BF16),
                        pltpu.VMEM((n_l, W_HEADS), BF16)] + _dsa_scratch(t, n_l),
        compiler_params=_cparams(1),
        name="dsa_sample",
    )(slab2, misc, ikd_past, cache_k, cache_v, slab2, slab2, slab2, slab2)


def _layernorm(z, g_ref, b_ref):
    mu = jnp.mean(z, axis=1, keepdims=True)
    d = z - mu
    var = jnp.mean(jnp.square(d), axis=1, keepdims=True)
    return d * lax.rsqrt(var + LN_EPS) * g_ref[...] + b_ref[...]


def _merge_kernel(x_ref, ya_ref, yb_ref, yc_ref, wgl_ref, bg_ref, wa_ref, wb_ref, wc_ref, wo_ref, g_ref, b_ref, o_ref):
    x = x_ref[...]
    xb = x.astype(BF16)
    merged = None
    for i, (y_ref, w_ref) in enumerate(((ya_ref, wa_ref), (yb_ref, wb_ref), (yc_ref, wc_ref))):
        gl = jnp.dot(xb, wgl_ref[:, i * D_MODEL:(i + 1) * D_MODEL], preferred_element_type=F32)
        gate = jax.nn.sigmoid(gl + bg_ref[i:i + 1, :])
        term = gate * jnp.dot(y_ref[...], w_ref[...], preferred_element_type=F32)
        merged = term if merged is None else merged + term
    z = DEEPNORM_ALPHA * x + jnp.dot(merged.astype(BF16), wo_ref[...], preferred_element_type=F32)
    o_ref[...] = _layernorm(z, g_ref, b_ref)


def _resident(shape, layer):
    nd = len(shape)
    return pl.BlockSpec((None,) + tuple(shape), lambda i: (layer,) + (0,) * nd, pipeline_mode=pl.Buffered(1))


def _merge(x, ya, yb, yc, wts, layer):
    m = x.shape[0]
    tm = min(TM_DENSE, m)
    row = lambda w: pl.BlockSpec((tm, w), lambda i: (i, 0))
    return pl.pallas_call(
        _merge_kernel,
        grid=(m // tm,),
        in_specs=[
            row(D_MODEL), row(W_HEADS), row(W_HEADS), row(W_HEADS),
            _resident((D_MODEL, N_BRANCH * D_MODEL), layer),
            _resident((N_BRANCH, D_MODEL), layer),
            _resident((W_HEADS, D_MODEL), layer),
            _resident((W_HEADS, D_MODEL), layer),
            _resident((W_HEADS, D_MODEL), layer),
            _resident((D_MODEL, D_MODEL), layer),
            _resident((1, D_MODEL), layer),
            _resident((1, D_MODEL), layer),
        ],
        out_specs=row(D_MODEL),
        out_shape=jax.ShapeDtypeStruct((m, D_MODEL), F32),
        compiler_params=_cparams(1),
        name="merge",
    )(x, ya, yb, yc, wts["w_gl"], wts["b_gate"], wts["w_br_a"], wts["w_br_b"], wts["w_br_c"], wts["w_o"],
      wts["ln1_g"], wts["ln1_b"])


def _ffn_kernel(x_ref, w1_ref, w3_ref, w2_ref, g_ref, b_ref, o_ref, *, d_ff, fc):
    x = x_ref[...]
    xb = x.astype(BF16)
    acc = None
    for c in range(d_ff // fc):
        sl = slice(c * fc, (c + 1) * fc)
        h1 = jnp.dot(xb, w1_ref[:, sl], preferred_element_type=F32)
        h3 = jnp.dot(xb, w3_ref[:, sl], preferred_element_type=F32)
        u = (h1 * jax.nn.sigmoid(h1) * h3).astype(BF16)
        term = jnp.dot(u, w2_ref[sl, :], preferred_element_type=F32)
        acc = term if acc is None else acc + term
    o_ref[...] = _layernorm(DEEPNORM_ALPHA * x + acc, g_ref, b_ref)


def _ffn(x, wts, layer):
    m = x.shape[0]
    tm = min(TM_DENSE, m)
    d_ff = wts["w_ff1"].shape[2]
    row = pl.BlockSpec((tm, D_MODEL), lambda i: (i, 0))
    return pl.pallas_call(
        functools.partial(_ffn_kernel, d_ff=d_ff, fc=256),
        grid=(m // tm,),
        in_specs=[
            row,
            _resident((D_MODEL, d_ff), layer),
            _resident((D_MODEL, d_ff), layer),
            _resident((d_ff, D_MODEL), layer),
            _resident((1, D_MODEL), layer),
            _resident((1, D_MODEL), layer),
        ],
        out_specs=row,
        out_shape=jax.ShapeDtypeStruct((m, D_MODEL), F32),
        compiler_params=_cparams(1),
        name="ffn",
    )(x, wts["w_ff1"], wts["w_ff3"], wts["w_ff2"], wts["ln2_g"], wts["ln2_b"])


def _rope_tables(pos, tm):
    half = HEAD_DIM // 2
    inv = ROPE_THETA ** (-jnp.arange(half, dtype=F32) / half)
    ang = pos.astype(F32)[:, None] * inv[None, :]
    cos, sin = jnp.cos(ang), jnp.sin(ang)
    cos_t = jnp.concatenate([cos, cos, cos, cos], axis=1)
    sin_t = jnp.concatenate([-sin, sin, -sin, sin], axis=1)
    reps = max(1, tm // pos.shape[0])
    return jnp.tile(cos_t, (reps, 1)), jnp.tile(sin_t, (reps, 1))


def _prep_weights(w_in, b_fgate, b_gate, diff_norm_g, w_br_a, w_br_b, w_br_c, w_o, ln1_g, ln1_b, ln2_g, ln2_b,
                  w_ff1, w_ff3, w_ff2, lam_q1, lam_k1, lam_q2, lam_k2):
    depth = w_in.shape[0]
    offs = np.cumsum((0,) + SPLIT_SIZES)
    col = lambda i: w_in[:, :, offs[i]:offs[i + 1]]
    aq, ak, av, af, bq, bk, bv, cq, ck, cv, iq, ik, iw, gl = (col(i) for i in range(len(SPLIT_SIZES)))
    qs = HEAD_DIM ** -0.5
    misc_pad = jnp.zeros((depth, D_MODEL, W_HEADS - D_IDX - H_FOX - H_IDX), w_in.dtype)
    w_slab = jnp.concatenate([aq * qs, ak, av, bq * qs, bk, bv, cq * qs, ck, cv, iq * (D_IDX ** -0.5),
                              ik, af, iw, misc_pad], axis=2).astype(BF16)
    bf_row = jnp.zeros((depth, 1, LANES), F32).at[:, 0, MISC_LOGF:MISC_LOGF + H_FOX].set(b_fgate.astype(F32))
    return {
        "w_slab": w_slab,
        "bf_row": bf_row,
        "w_gl": gl.astype(BF16),
        "b_gate": b_gate.astype(F32),
        "w_br_a": w_br_a.astype(BF16), "w_br_b": w_br_b.astype(BF16), "w_br_c": w_br_c.astype(BF16),
        "w_o": w_o.astype(BF16),
        "ln1_g": ln1_g[:, None, :].astype(F32), "ln1_b": ln1_b[:, None, :].astype(F32),
        "ln2_g": ln2_g[:, None, :].astype(F32), "ln2_b": ln2_b[:, None, :].astype(F32),
        "w_ff1": w_ff1.astype(BF16), "w_ff3": w_ff3.astype(BF16), "w_ff2": w_ff2.astype(BF16),
        "lam4": jnp.stack([lam_q1, lam_k1, lam_q2, lam_k2], axis=1).astype(F32),
        "diff_g": diff_norm_g[:, None, :].astype(F32),
    }


def _layer_group(x, wts, layer, tabs, bsz, t, caches):
    cos_t, sin_t = tabs
    slab, rows32, misc = _inproj(x, wts["w_slab"], layer, cos_t, sin_t, wts["bf_row"], t)
    logf = misc[:, MISC_LOGF:MISC_LOGF + H_FOX].reshape(bsz, t, H_FOX)
    lf_all = misc.reshape(bsz, t, LANES)
    past = 0
    if caches is not None:
        past = caches["a_logf"].shape[2]
        past_lf = jnp.pad(caches["a_logf"][layer].astype(F32),
                          ((0, 0), (0, 0), (MISC_LOGF, LANES - MISC_LOGF - H_FOX)))
        lf_all = jnp.concatenate([past_lf, lf_all], axis=1)
    n_keys = past + t
    cum = _cumsum_time(lf_all)[:, :, MISC_LOGF:MISC_LOGF + H_FOX]
    npair = H_FOX // 2
    ck4 = cum.transpose(0, 2, 1).reshape(bsz, npair, 2, n_keys)
    cq4 = cum[:, past:].reshape(bsz, t, npair, 2).transpose(0, 2, 1, 3)
    lam_init = 0.8 - 0.6 * math.exp(-0.3 * layer)
    diff_extra = (wts["lam4"][layer], wts["diff_g"][layer])
    k_sel = min(TOPK_MAX, n_keys // 4)
    if caches is None:
        slab3 = slab.reshape(bsz, t, SLAB_W)
        ya = _pair_attn_prompt("fox", slab, slab3, G_AQ, G_AK, G_AV, (cq4, ck4), lam_init)
        yb = _pair_attn_prompt("diff", slab, slab3, G_BQ, G_BK, G_BV, diff_extra, lam_init)
        yc = _dsa_prompt(slab, slab3, misc, k_sel)
    else:
        ya = _pair_attn_sample("fox", slab, t, caches["a_k"], caches["a_v"], layer, G_AQ, G_AK, G_AV,
                               (cq4, ck4), lam_init)
        yb = _pair_attn_sample("diff", slab, t, caches["b_k"], caches["b_v"], layer, G_BQ, G_BK, G_BV,
                               diff_extra, lam_init)
        yc = _dsa_sample(slab, misc, t, caches["c_ikd"], caches["c_k"], caches["c_v"], layer, k_sel)
    x = _merge(x, ya, yb, yc, wts, layer)
    x = _ffn(x, wts, layer)
    g = lambda i: rows32[:, i * W_HEADS:(i + 1) * W_HEADS]
    new_rows = (
        g(0).reshape(bsz, t, H_FOX, HEAD_DIM), g(1).reshape(bsz, t, H_FOX, HEAD_DIM), logf,
        g(2).reshape(bsz, t, H_DIFF, 2, HEAD_DIM), g(3).reshape(bsz, t, H_DIFF, 2 * HEAD_DIM),
        g(4).reshape(bsz, t, H_DSA, HEAD_DIM), g(5).reshape(bsz, t, H_DSA, HEAD_DIM),
        misc[:, :D_IDX].reshape(bsz, t, D_IDX),
    )
    return x, new_rows


def kernel(x_prompt, x_sample, cache_a_k, cache_a_v, cache_a_logf, cache_b_k, cache_b_v, cache_c_k, cache_c_v, cache_c_idx, w_in, b_fgate, b_gate, lam_q1, lam_k1, lam_q2, lam_k2, diff_norm_g, w_br_a, w_br_b, w_br_c, w_o, ln1_g, ln1_b, ln2_g, ln2_b, w_ff1, w_ff3, w_ff2):
    depth = w_in.shape[0]
    bp, tp, _ = x_prompt.shape
    bs, ts, _ = x_sample.shape
    past = cache_a_k.shape[2]
    wts = _prep_weights(w_in, b_fgate, b_gate, diff_norm_g, w_br_a, w_br_b, w_br_c, w_o, ln1_g, ln1_b, ln2_g, ln2_b,
                        w_ff1, w_ff3, w_ff2, lam_q1, lam_k1, lam_q2, lam_k2)
    tabs_p = _rope_tables(jnp.arange(tp, dtype=I32), min(TM_DENSE, bp * tp))
    tabs_s = _rope_tables(past + jnp.arange(ts, dtype=I32), min(TM_DENSE, bs * ts))
    flat = lambda c: c.reshape(c.shape[0], c.shape[1], c.shape[2], -1)
    caches = {
        "a_k": flat(cache_a_k), "a_v": flat(cache_a_v), "a_logf": cache_a_logf,
        "b_k": flat(cache_b_k), "b_v": flat(cache_b_v),
        "c_k": flat(cache_c_k), "c_v": flat(cache_c_v),
        "c_ikd": jnp.concatenate([cache_c_idx, cache_c_idx], axis=-1).astype(BF16),
    }
    yp = x_prompt.reshape(bp * tp, D_MODEL)
    ys = x_sample.reshape(bs * ts, D_MODEL)
    rows_p, rows_s = [], []
    for layer in range(depth):
        yp, rp = _layer_group(yp, wts, layer, tabs_p, bp, tp, None)
        ys, rs = _layer_group(ys, wts, layer, tabs_s, bs, ts, caches)
        rows_p.append(rp)
        rows_s.append(rs)
    outs_p = [jnp.stack([r[i] for r in rows_p]) for i in range(8)]
    outs_s = [jnp.stack([r[i] for r in rows_s]) for i in range(8)]
    return (yp.reshape(bp, tp, D_MODEL), ys.reshape(bs, ts, D_MODEL), *outs_p, *outs_s)
```

```python
import functools
import math

import jax
import jax.numpy as jnp
import numpy as np
from jax import lax
from jax.experimental import pallas as pl
from jax.experimental.pallas import tpu as pltpu

F32 = jnp.float32
BF16 = jnp.bfloat16
I32 = jnp.int32

D_MODEL = 1024
HEAD_DIM = 64
H_FOX = 8
H_DIFF = 4
H_DSA = 8
H_IDX = 8
D_IDX = 64
CHUNK = 64
CHUNK_SHIFT = 6
TOPK_MAX = 256
ROPE_THETA = 10000.0
N_BRANCH = 3
LN_EPS = 1e-5
MODEL_DEPTH = 4
DEEPNORM_ALPHA = (2 * MODEL_DEPTH) ** 0.25
W_HEADS = 512
SPLIT_SIZES = (W_HEADS, W_HEADS, W_HEADS, H_FOX, W_HEADS, W_HEADS, W_HEADS, W_HEADS, W_HEADS, W_HEADS,
               H_IDX * D_IDX, D_IDX, H_IDX, N_BRANCH * D_MODEL)

LANES = 128
N_GROUPS = 11
SLAB_W = N_GROUPS * W_HEADS
G_AQ, G_AK, G_AV, G_BQ, G_BK, G_BV, G_CQ, G_CK, G_CV, G_IQ, G_MISC = range(N_GROUPS)
MISC_LOGF = 64
MISC_IW = 72
VMEM_LIMIT = 56 * 1024 * 1024
NEG_INF = float("-inf")
INT_MIN = np.int32(-2 ** 31)
INT_MAX_MASK = np.int32(2 ** 31 - 1)

TM_DENSE = 512
TQ_ATTN = 512
TQ_DSA = 512
SEG_DSA = 512
CH_DSA = 512


def _cparams(n_axes):
    return pltpu.CompilerParams(dimension_semantics=("arbitrary",) * n_axes, vmem_limit_bytes=VMEM_LIMIT)


def _lane_iota(n=LANES):
    return lax.broadcasted_iota(I32, (1, n), 1)


def _f32_group(j):
    return jnp.minimum(2 * (j // 3) + (j % 3) // 2, 5)


def _inproj_kernel(x_ref, w_ref, cos_ref, sin_ref, bf_ref, s16_ref, s32_ref, misc_ref, *rest):
    if len(rest) == 2:
        vt_ref, xb_ref = rest
    else:
        vt_ref, (xb_ref,) = None, rest
    j = pl.program_id(1)

    @pl.when(j == 0)
    def _():
        xb_ref[...] = x_ref[...].astype(BF16)

    h = jnp.dot(xb_ref[...], w_ref[...], preferred_element_type=F32)
    lane = _lane_iota()
    sel_up = (lane & (HEAD_DIM - 1)) < HEAD_DIM // 2

    def rope128(xb):
        up = pltpu.roll(xb, LANES - HEAD_DIM // 2, 1)
        dn = pltpu.roll(xb, HEAD_DIM // 2, 1)
        return xb * cos_ref[...] + jnp.where(sel_up, up, dn) * sin_ref[...]

    is_rope = (j == G_BQ) | (j == G_BK) | (j == G_CQ) | (j == G_CK) | (j == G_IQ)
    has32 = (j == G_AK) | (j == G_AV) | (j == G_BK) | (j == G_BV) | (j == G_CK) | (j == G_CV)

    def write(val_fn):
        for c in range(W_HEADS // LANES):
            sl = slice(c * LANES, (c + 1) * LANES)
            v = val_fn(h[:, sl])
            s16_ref[:, sl] = v.astype(BF16)

            @pl.when(has32)
            def _():
                s32_ref[:, sl] = v

    @pl.when(is_rope)
    def _():
        write(rope128)

    @pl.when(jnp.logical_not(is_rope) & (j < G_MISC))
    def _():
        write(lambda v: v)

    if vt_ref is not None:
        @pl.when((j == G_AV) | (j == G_BV) | (j == G_CV))
        def _():
            vt_ref[0] = h.T.astype(BF16)

    @pl.when(j == G_MISC)
    def _():
        hb = h[:, :LANES]
        roped = rope128(hb)
        z = hb + bf_ref[...]
        logf = jnp.minimum(z, 0.0) - jnp.log1p(jnp.exp(-jnp.abs(z)))
        misc_ref[...] = jnp.where(lane < MISC_LOGF, roped, jnp.where(lane < MISC_IW, logf, hb))
        ikd = jnp.where(lane < D_IDX, roped, pltpu.roll(roped, D_IDX, 1))
        s16_ref[:, :LANES] = ikd.astype(BF16)
        s16_ref[:, LANES:] = jnp.zeros((s16_ref.shape[0], W_HEADS - LANES), BF16)


def _inproj(x, w_all, layer, cos_t, sin_t, bf_row, seq, with_vt):
    m = x.shape[0]
    tm = min(TM_DENSE, m)
    n_tab = cos_t.shape[0] // tm
    tab_map = lambda i, j: (i % n_tab, 0)
    out_specs = [
        pl.BlockSpec((tm, W_HEADS), lambda i, j: (i, j)),
        pl.BlockSpec((tm, W_HEADS), lambda i, j: (i, _f32_group(j))),
        pl.BlockSpec((tm, LANES), lambda i, j: (i, 0)),
    ]
    out_shape = [
        jax.ShapeDtypeStruct((m, SLAB_W), BF16),
        jax.ShapeDtypeStruct((m, 6 * W_HEADS), F32),
        jax.ShapeDtypeStruct((m, LANES), F32),
    ]
    if with_vt:
        nt = seq // tm
        out_specs.append(pl.BlockSpec((1, W_HEADS, tm), lambda i, j: (i // nt, jnp.minimum(j // 3, 2), i % nt)))
        out_shape.append(jax.ShapeDtypeStruct((m // seq, 3 * W_HEADS, seq), BF16))
    return pl.pallas_call(
        _inproj_kernel,
        grid=(m // tm, N_GROUPS),
        in_specs=[
            pl.BlockSpec((tm, D_MODEL), lambda i, j: (i, 0)),
            pl.BlockSpec((None, D_MODEL, W_HEADS), lambda i, j: (layer, 0, j)),
            pl.BlockSpec((tm, LANES), tab_map),
            pl.BlockSpec((tm, LANES), tab_map),
            pl.BlockSpec((None, 1, LANES), lambda i, j: (layer, 0, 0)),
        ],
        out_specs=out_specs,
        out_shape=out_shape,
        scratch_shapes=[pltpu.VMEM((tm, D_MODEL), BF16)],
        compiler_params=_cparams(2),
        name="inproj",
    )(x, w_all, cos_t, sin_t, bf_row)


def _split3(x):
    hi = x.astype(BF16)
    r1 = x - hi.astype(F32)
    mid = r1.astype(BF16)
    lo = (r1 - mid.astype(F32)).astype(BF16)
    return hi, mid, lo


def _cumsum_kernel(x_ref, o_ref, carry_ref):
    t = pl.program_id(1)

    @pl.when(t == 0)
    def _():
        carry_ref[...] = jnp.zeros_like(carry_ref)

    x = x_ref[0]
    tl = x.shape[0]
    r = lax.broadcasted_iota(I32, (tl, tl), 0)
    c = lax.broadcasted_iota(I32, (tl, tl), 1)
    tri = jnp.where(c <= r, 1.0, 0.0).astype(BF16)
    hi, mid, lo = _split3(x)
    dot = lambda a: jnp.dot(tri, a, preferred_element_type=F32)
    cs = (dot(lo) + dot(mid)) + dot(hi) + carry_ref[...]
    o_ref[0] = cs
    carry_ref[...] = cs[tl - 1:tl, :]


def _cumsum_time(x):
    b, l, h = x.shape
    tl = 512 if l % 512 == 0 else l
    return pl.pallas_call(
        _cumsum_kernel,
        grid=(b, l // tl),
        in_specs=[pl.BlockSpec((1, tl, h), lambda i, t: (i, t, 0))],
        out_specs=pl.BlockSpec((1, tl, h), lambda i, t: (i, t, 0)),
        out_shape=jax.ShapeDtypeStruct((b, l, h), F32),
        scratch_shapes=[pltpu.VMEM((1, h), F32)],
        compiler_params=_cparams(2),
        name="cumsum",
    )(x)


def _split_pair(q):
    lane = _lane_iota()
    zero = jnp.zeros_like(q)
    return jnp.where(lane < HEAD_DIM, q, zero), jnp.where(lane >= HEAD_DIM, q, zero)


def _qk(q, k):
    return lax.dot_general(q, k, (((1,), (1,)), ((), ())), preferred_element_type=F32)


def _online_update(s, v, m_old, l_old, acc_old):
    m_new = jnp.maximum(m_old, jnp.max(s, axis=1, keepdims=True))
    m_safe = jnp.where(m_new == NEG_INF, 0.0, m_new)
    alpha = jnp.exp(m_old - m_safe)
    p = jnp.exp(s - m_safe)
    l_new = alpha * l_old + jnp.sum(p, axis=1, keepdims=True)
    acc_new = alpha * acc_old + jnp.dot(p.astype(BF16), v, preferred_element_type=F32)
    return m_new, l_new, acc_new


def _causal_mask(kind, rows, cols, row0, col0):
    r = row0 + lax.broadcasted_iota(I32, (rows, 1), 0)
    c = col0 + lax.broadcasted_iota(I32, (1, cols), 1)
    if kind == "fox":
        return c <= r
    return (c >> CHUNK_SHIFT) <= (r >> CHUNK_SHIFT)


def _diff_lambda(lam_ref, lam_init):
    lp = lam_ref[...]
    s1 = jnp.sum(lp[0:1] * lp[1:2], axis=1, keepdims=True)
    s2 = jnp.sum(lp[2:3] * lp[3:4], axis=1, keepdims=True)
    return jnp.exp(s1) - jnp.exp(s2) + lam_init


def _pair_epilogue(kind, o0, o1, lam_ref, g_ref, lam_init):
    if kind == "fox":
        return jnp.where(_lane_iota() < HEAD_DIM, o0, o1)
    o = o0 - _diff_lambda(lam_ref, lam_init) * o1
    o = o * lax.rsqrt(jnp.mean(jnp.square(o), axis=1, keepdims=True) + LN_EPS) * g_ref[...]
    return o * (1.0 - lam_init)


def _online_update_t(st, vt, m_old, l_old, acc_old):
    m_new = jnp.maximum(m_old, jnp.max(st, axis=0, keepdims=True))
    m_safe = jnp.where(m_new == NEG_INF, 0.0, m_new)
    alpha = jnp.exp(m_old - m_safe)
    pt = jnp.exp(st - m_safe)
    l_new = alpha * l_old + jnp.sum(pt, axis=0, keepdims=True)
    acc_new = alpha * acc_old + jnp.dot(vt, pt.astype(BF16), preferred_element_type=F32)
    return m_new, l_new, acc_new


def _causal_mask_t(kind, keys, queries):
    kp = lax.broadcasted_iota(I32, (keys, 1), 0)
    qp = lax.broadcasted_iota(I32, (1, queries), 1)
    if kind == "fox":
        return kp <= qp
    return (kp >> CHUNK_SHIFT) <= (qp >> CHUNK_SHIFT)


def _init_stats(m_ref, l_ref, acc_ref):
    m_ref[...] = jnp.full(m_ref.shape, NEG_INF, F32)
    l_ref[...] = jnp.zeros(l_ref.shape, F32)
    acc_ref[...] = jnp.zeros(acc_ref.shape, F32)


def _pair_attn_prompt_kernel(*refs, kind, tq, lam_init):
    if kind == "fox":
        q_ref, k_ref, vt_ref, cq_ref, ck_ref, o_ref, m_ref, l_ref, acc_ref = refs
        lam_ref = g_ref = None
    else:
        q_ref, k_ref, vt_ref, lam_ref, g_ref, o_ref, m_ref, l_ref, acc_ref = refs
    qi = pl.program_id(2)
    qs = _split_pair(q_ref[...])
    _init_stats(m_ref, l_ref, acc_ref)

    def step(j, masked):
        start = pl.multiple_of(j * tq, tq)
        k = k_ref[0, pl.ds(start, tq), :]
        vt = vt_ref[0, :, pl.ds(start, tq)]
        for h in range(2):
            st = _qk(k, qs[h])
            if kind == "fox":
                st = st + cq_ref[0, 0, h:h + 1, :] - ck_ref[0, 0, pl.ds(start, tq), h:h + 1]
            if masked:
                st = jnp.where(_causal_mask_t(kind, tq, tq), st, NEG_INF)
            m_ref[h], l_ref[h], acc_ref[h] = _online_update_t(st, vt, m_ref[h], l_ref[h], acc_ref[h])

    def body(j, carry):
        step(j, False)
        return carry

    lax.fori_loop(0, qi, body, 0)
    step(qi, True)
    o0 = (acc_ref[0] * (1.0 / l_ref[0])).T
    o1 = (acc_ref[1] * (1.0 / l_ref[1])).T
    o_ref[...] = _pair_epilogue(kind, o0, o1, lam_ref, g_ref, lam_init).astype(BF16)


def _pair_attn_prompt(kind, slab2, slab3, vt_slab, qg, kg, vt_group, extra, lam_init):
    b, t, _ = slab3.shape
    tq = min(TQ_ATTN, t)
    nq = t // tq
    npair = W_HEADS // LANES
    in_specs = [
        pl.BlockSpec((tq, LANES), lambda i, p, q: (i * nq + q, qg * npair + p)),
        pl.BlockSpec((1, t, LANES), lambda i, p, q: (i, 0, kg * npair + p)),
        pl.BlockSpec((1, LANES, t), lambda i, p, q: (i, vt_group * npair + p, 0)),
    ]
    if kind == "fox":
        in_specs += [
            pl.BlockSpec((1, 1, 2, tq), lambda i, p, q: (i, p, 0, q)),
            pl.BlockSpec((1, 1, t, 2), lambda i, p, q: (i, p, 0, 0)),
        ]
    else:
        in_specs += [
            pl.BlockSpec((4, HEAD_DIM), lambda i, p, q: (0, 0)),
            pl.BlockSpec((1, LANES), lambda i, p, q: (0, 0)),
        ]
    return pl.pallas_call(
        functools.partial(_pair_attn_prompt_kernel, kind=kind, tq=tq, lam_init=lam_init),
        grid=(b, npair, nq),
        in_specs=in_specs,
        out_specs=pl.BlockSpec((tq, LANES), lambda i, p, q: (i * nq + q, p)),
        out_shape=jax.ShapeDtypeStruct((b * t, W_HEADS), BF16),
        scratch_shapes=[pltpu.VMEM((2, 1, tq), F32), pltpu.VMEM((2, 1, tq), F32), pltpu.VMEM((2, LANES, tq), F32)],
        compiler_params=_cparams(3),
        name=kind + "_prompt",
    )(slab2, slab3, vt_slab, *extra)


def _pair_attn_sample_kernel(*refs, kind, t, past, lam_init):
    if kind == "fox":
        q_ref, kp_ref, vp_ref, kn_ref, vn_ref, cq_ref, ck_ref, o_ref = refs
        lam_ref = g_ref = None
    else:
        q_ref, kp_ref, vp_ref, kn_ref, vn_ref, lam_ref, g_ref, o_ref = refs
    qs = _split_pair(q_ref[...])
    kp = kp_ref[...].astype(BF16)
    vp = vp_ref[...].astype(BF16)
    kn = kn_ref[...]
    vn = vn_ref[...]
    outs = []
    for h in range(2):
        m = jnp.full((t, 1), NEG_INF, F32)
        l = jnp.zeros((t, 1), F32)
        acc = jnp.zeros((t, LANES), F32)
        s = _qk(qs[h], kp)
        if kind == "fox":
            s = s + cq_ref[0, 0, :, h:h + 1] - ck_ref[0, 0, h:h + 1, :past]
        m, l, acc = _online_update(s, vp, m, l, acc)
        s = _qk(qs[h], kn)
        if kind == "fox":
            s = s + cq_ref[0, 0, :, h:h + 1] - ck_ref[0, 0, h:h + 1, past:past + t]
        s = jnp.where(_causal_mask(kind, t, t, past, past), s, NEG_INF)
        m, l, acc = _online_update(s, vn, m, l, acc)
        outs.append(acc * (1.0 / l))
    o_ref[...] = _pair_epilogue(kind, outs[0], outs[1], lam_ref, g_ref, lam_init).astype(BF16)


def _pair_attn_sample(kind, slab2, t, cache_k, cache_v, layer, qg, kg, vg, extra, lam_init):
    b = slab2.shape[0] // t
    past = cache_k.shape[2]
    npair = W_HEADS // LANES
    in_specs = [
        pl.BlockSpec((t, LANES), lambda i, p: (i, qg * npair + p)),
        pl.BlockSpec((None, None, past, LANES), lambda i, p: (layer, i, 0, p)),
        pl.BlockSpec((None, None, past, LANES), lambda i, p: (layer, i, 0, p)),
        pl.BlockSpec((t, LANES), lambda i, p: (i, kg * npair + p)),
        pl.BlockSpec((t, LANES), lambda i, p: (i, vg * npair + p)),
    ]
    if kind == "fox":
        in_specs += [
            pl.BlockSpec((1, 1, t, 2), lambda i, p: (i, p, 0, 0)),
            pl.BlockSpec((1, 1, 2, past + t), lambda i, p: (i, p, 0, 0)),
        ]
    else:
        in_specs += [
            pl.BlockSpec((4, HEAD_DIM), lambda i, p: (0, 0)),
            pl.BlockSpec((1, LANES), lambda i, p: (0, 0)),
        ]
    return pl.pallas_call(
        functools.partial(_pair_attn_sample_kernel, kind=kind, t=t, past=past, lam_init=lam_init),
        grid=(b, npair),
        in_specs=in_specs,
        out_specs=pl.BlockSpec((t, LANES), lambda i, p: (i, p)),
        out_shape=jax.ShapeDtypeStruct((b * t, W_HEADS), BF16),
        compiler_params=_cparams(2),
        name=kind + "_sample",
    )(slab2, cache_k, cache_v, slab2, slab2, *extra)


def _dsa_core(iq_ref, iw, ikd_ref, cq_ref, ck_ref, cv_ref, o_ref, keys_ref, bias_ref, m_ref, l_ref, acc_ref,
              *, tq, n_l, ch, qpos0, n_keys, k_sel):
    nch = n_l // ch
    lane = _lane_iota()
    qchunk = (qpos0 + lax.broadcasted_iota(I32, (tq, 1), 0)) >> CHUNK_SHIFT
    w = [iw[:, MISC_IW + h:MISC_IW + h + 1] for h in range(H_IDX)]

    def score_chunk(c, carry):
        start = pl.multiple_of(c * ch, ch)
        ik = ikd_ref[pl.ds(start, ch), :]
        acc = jnp.zeros((tq, ch), F32)
        for hp in range(H_IDX // 2):
            pair = _split_pair(iq_ref[:, hp * LANES:(hp + 1) * LANES])
            for half in range(2):
                acc = acc + jnp.maximum(_qk(pair[half], ik), 0.0) * w[2 * hp + half]
        score = acc * (H_IDX ** -0.5) + 0.0
        bits = lax.bitcast_convert_type(score, I32)
        key = bits ^ ((bits >> 31) & INT_MAX_MASK)
        kpos = start + lax.broadcasted_iota(I32, (1, ch), 1)
        adm = ((kpos >> CHUNK_SHIFT) <= qchunk) & (kpos < n_keys)
        keys_ref[:, pl.ds(start, ch)] = jnp.where(adm, key, INT_MIN)
        return carry

    lax.fori_loop(0, nch, score_chunk, 0)

    def count(pred):
        def body(c, a):
            start = pl.multiple_of(c * ch, ch)
            hit = jnp.where(pred(keys_ref[:, pl.ds(start, ch)]), 1.0, 0.0)
            for t in range(ch // LANES):
                a = a + hit[:, t * LANES:(t + 1) * LANES]
            return a
        part = lax.fori_loop(0, nch, body, jnp.zeros((tq, LANES), F32))
        return jnp.sum(part, axis=1, keepdims=True)

    def bit_step(i, thr_u):
        cand_u = thr_u | lax.shift_left(jnp.int32(1), 31 - i)
        cand_s = cand_u ^ INT_MIN
        n_ge = count(lambda kk: kk >= cand_s)
        return jnp.where(n_ge >= k_sel, cand_u, thr_u)

    thr = lax.fori_loop(0, 32, bit_step, jnp.zeros((tq, 1), I32)) ^ INT_MIN
    need = k_sel - count(lambda kk: kk > thr)

    r128 = lax.broadcasted_iota(I32, (LANES, LANES), 0)
    c128 = lax.broadcasted_iota(I32, (LANES, LANES), 1)
    before = jnp.where(r128 < c128, 1.0, 0.0).astype(BF16)

    def select_block(t, seen):
        start = pl.multiple_of(t * LANES, LANES)
        kk = keys_ref[:, pl.ds(start, LANES)]
        eq = kk == thr
        eqf = jnp.where(eq, 1.0, 0.0)
        rank = seen + jnp.dot(eqf.astype(BF16), before, preferred_element_type=F32)
        sel = ((kk > thr) | (eq & (rank < need))) & (kk != INT_MIN)
        bias_ref[:, pl.ds(start, LANES)] = jnp.where(sel, 0.0, NEG_INF)
        return seen + jnp.sum(eqf, axis=1, keepdims=True)

    lax.fori_loop(0, n_l // LANES, select_block, jnp.zeros((tq, 1), F32))

    for p in range(H_DSA // 2):
        sl = slice(p * LANES, (p + 1) * LANES)
        qs = _split_pair(cq_ref[:, sl])
        m_ref[...] = jnp.full(m_ref.shape, NEG_INF, F32)
        l_ref[...] = jnp.zeros(l_ref.shape, F32)
        acc_ref[...] = jnp.zeros(acc_ref.shape, F32)

        def attend(c, carry):
            start = pl.multiple_of(c * ch, ch)
            k = ck_ref[pl.ds(start, ch), sl]
            v = cv_ref[pl.ds(start, ch), sl]
            bias = bias_ref[:, pl.ds(start, ch)]
            for h in range(2):
                s = _qk(qs[h], k) + bias
                m_ref[h], l_ref[h], acc_ref[h] = _online_update(s, v, m_ref[h], l_ref[h], acc_ref[h])
            return carry

        lax.fori_loop(0, nch, attend, 0)
        o0 = acc_ref[0] * (1.0 / l_ref[0])
        o1 = acc_ref[1] * (1.0 / l_ref[1])
        o_ref[:, sl] = jnp.where(lane < HEAD_DIM, o0, o1).astype(BF16)


def _dsa_scratch(tq, n_l):
    return [pltpu.VMEM((tq, n_l), I32), pltpu.VMEM((tq, n_l), F32),
            pltpu.VMEM((2, tq, 1), F32), pltpu.VMEM((2, tq, 1), F32), pltpu.VMEM((2, tq, LANES), F32)]


def _dsa_core_t(iq_ref, iw_t, ikd_ref, cq_ref, ck_ref, cvt_ref, o_ref, keys_ref, bias_ref, m_ref, l_ref, acc_ref,
                *, tq, n_l, ch, qpos0, n_keys, k_sel):
    nch = n_l // ch
    qchunk = (qpos0 + _lane_iota(tq)) >> CHUNK_SHIFT
    w = [iw_t[MISC_IW + h:MISC_IW + h + 1, :] for h in range(H_IDX)]
    iqs = []
    for hp in range(H_IDX // 2):
        iqs.extend(_split_pair(iq_ref[:, hp * LANES:(hp + 1) * LANES]))

    def score_chunk(c, carry):
        start = pl.multiple_of(c * ch, ch)
        ik = ikd_ref[pl.ds(start, ch), :]
        acc = jnp.zeros((ch, tq), F32)
        for h in range(H_IDX):
            acc = acc + jnp.maximum(_qk(ik, iqs[h]), 0.0) * w[h]
        score = acc * (H_IDX ** -0.5) + 0.0
        bits = lax.bitcast_convert_type(score, I32)
        key = bits ^ ((bits >> 31) & INT_MAX_MASK)
        kpos = start + lax.broadcasted_iota(I32, (ch, 1), 0)
        adm = ((kpos >> CHUNK_SHIFT) <= qchunk) & (kpos < n_keys)
        keys_ref[pl.ds(start, ch), :] = jnp.where(adm, key, INT_MIN)
        return carry

    lax.fori_loop(0, nch, score_chunk, 0)

    def count(pred):
        def body(c, a):
            start = pl.multiple_of(c * ch, ch)
            hit = jnp.where(pred(keys_ref[pl.ds(start, ch), :]), 1.0, 0.0)
            return a + jnp.sum(hit.reshape(ch // 8, 8, tq), axis=0)
        part = lax.fori_loop(0, nch, body, jnp.zeros((8, tq), F32))
        return jnp.sum(part, axis=0, keepdims=True)

    few = count(lambda kk: kk != INT_MIN) <= k_sel

    def search_on(state):
        i, _, n_thr = state
        open_rows = jnp.where((n_thr == k_sel) | few, 0, 1)
        return (i < 32) & (jnp.max(open_rows) > 0)

    def bit_step(state):
        i, thr_u, n_thr = state
        cand_u = thr_u | lax.shift_left(jnp.int32(1), 31 - i)
        cand_s = cand_u ^ INT_MIN
        n_ge = count(lambda kk: kk >= cand_s)
        take = n_ge >= k_sel
        return i + 1, jnp.where(take, cand_u, thr_u), jnp.where(take, n_ge, n_thr)

    _, thr_u, _ = lax.while_loop(search_on, bit_step,
                                 (jnp.int32(0), jnp.zeros((1, tq), I32), jnp.full((1, tq), float(n_l), F32)))
    thr = thr_u ^ INT_MIN
    need = k_sel - count(lambda kk: kk > thr)

    sb = 2 * LANES
    r_i = lax.broadcasted_iota(I32, (sb, sb), 0)
    c_i = lax.broadcasted_iota(I32, (sb, sb), 1)
    before = jnp.where(c_i < r_i, 1.0, 0.0).astype(BF16)

    def select_block(t, seen):
        start = pl.multiple_of(t * sb, sb)
        kk = keys_ref[pl.ds(start, sb), :]
        eq = kk == thr
        eqf = jnp.where(eq, 1.0, 0.0)
        rank = seen + jnp.dot(before, eqf.astype(BF16), preferred_element_type=F32)
        sel = ((kk > thr) | (eq & (rank < need))) & (kk != INT_MIN)
        bias_ref[pl.ds(start, sb), :] = jnp.where(sel, 0.0, NEG_INF)
        return seen + jnp.sum(eqf, axis=0, keepdims=True)

    lax.fori_loop(0, n_l // sb, select_block, jnp.zeros((1, tq), F32))

    head0_rows = lax.broadcasted_iota(I32, (LANES, 1), 0) < HEAD_DIM
    for p in range(H_DSA // 2):
        sl = slice(p * LANES, (p + 1) * LANES)
        qs = _split_pair(cq_ref[:, sl])
        _init_stats(m_ref, l_ref, acc_ref)

        def attend(c, carry):
            start = pl.multiple_of(c * ch, ch)
            k = ck_ref[pl.ds(start, ch), sl]
            vt = cvt_ref[sl, pl.ds(start, ch)]
            bias = bias_ref[pl.ds(start, ch), :]
            for h in range(2):
                st = _qk(k, qs[h]) + bias
                m_ref[h], l_ref[h], acc_ref[h] = _online_update_t(st, vt, m_ref[h], l_ref[h], acc_ref[h])
            return carry

        lax.fori_loop(0, nch, attend, 0)
        ot = jnp.where(head0_rows, acc_ref[0] * (1.0 / l_ref[0]), acc_ref[1] * (1.0 / l_ref[1]))
        o_ref[:, sl] = ot.T.astype(BF16)


def _dsa_prompt_kernel(iq_ref, misc_ref, ikd_ref, cq_ref, ck_ref, cvt_ref, o_ref, *scratch, tq, n_l, q0, n_keys, k_sel):
    qpos0 = q0 + pl.program_id(1) * tq
    _dsa_core_t(iq_ref, misc_ref[...].T, ikd_ref.at[0], cq_ref, ck_ref.at[0], cvt_ref.at[0], o_ref, *scratch,
                tq=tq, n_l=n_l, ch=min(CH_DSA, n_l), qpos0=qpos0, n_keys=n_keys, k_sel=k_sel)


def _dsa_prompt_segment(slab2, slab3, vt_slab, misc, seg, seg_len, k_sel):
    b, t, _ = slab3.shape
    tq = min(TQ_DSA, seg_len)
    nqs = seg_len // tq
    nq = t // tq
    n_l = (seg + 1) * seg_len
    row = lambda i, q: i * nq + seg * nqs + q
    npair = W_HEADS // LANES
    return pl.pallas_call(
        functools.partial(_dsa_prompt_kernel, tq=tq, n_l=n_l, q0=seg * seg_len, n_keys=t, k_sel=k_sel),
        grid=(b, nqs),
        in_specs=[
            pl.BlockSpec((tq, W_HEADS), lambda i, q: (row(i, q), G_IQ)),
            pl.BlockSpec((tq, LANES), lambda i, q: (row(i, q), 0)),
            pl.BlockSpec((1, n_l, LANES), lambda i, q: (i, 0, G_MISC * npair)),
            pl.BlockSpec((tq, W_HEADS), lambda i, q: (row(i, q), G_CQ)),
            pl.BlockSpec((1, n_l, W_HEADS), lambda i, q: (i, 0, G_CK)),
            pl.BlockSpec((1, W_HEADS, n_l), lambda i, q: (i, 2, 0)),
        ],
        out_specs=pl.BlockSpec((tq, W_HEADS), lambda i, q: (i * nqs + q, 0)),
        out_shape=jax.ShapeDtypeStruct((b * seg_len, W_HEADS), BF16),
        scratch_shapes=[pltpu.VMEM((n_l, tq), I32), pltpu.VMEM((n_l, tq), F32),
                        pltpu.VMEM((2, 1, tq), F32), pltpu.VMEM((2, 1, tq), F32), pltpu.VMEM((2, LANES, tq), F32)],
        compiler_params=_cparams(2),
        name="dsa_prompt",
    )(slab2, misc, slab3, slab2, slab3, vt_slab)


def _dsa_prompt(slab2, slab3, vt_slab, misc, k_sel):
    b, t, _ = slab3.shape
    seg_len = min(SEG_DSA, t)
    segs = [_dsa_prompt_segment(slab2, slab3, vt_slab, misc, s, seg_len, k_sel) for s in range(t // seg_len)]
    y = jnp.stack([s.reshape(b, seg_len, W_HEADS) for s in segs], axis=1)
    return y.reshape(b * t, W_HEADS)


def _dsa_sample_kernel(iq_ref, misc_ref, ikp_ref, ckp_ref, cvp_ref, ikn_ref, cq_ref, ckn_ref, cvn_ref, o_ref,
                       ik_s, ck_s, cv_s, *scratch, t, past, n_l, ch, k_sel):
    pad = n_l - past - t
    ik_s[:past, :] = ikp_ref[...]
    ik_s[past:past + t, :] = ikn_ref[...]
    ik_s[past + t:, :] = jnp.zeros((pad, LANES), BF16)
    ck_s[:past, :] = ckp_ref[...].astype(BF16)
    ck_s[past:past + t, :] = ckn_ref[...]
    ck_s[past + t:, :] = jnp.zeros((pad, W_HEADS), BF16)
    cv_s[:past, :] = cvp_ref[...].astype(BF16)
    cv_s[past:past + t, :] = cvn_ref[...]
    cv_s[past + t:, :] = jnp.zeros((pad, W_HEADS), BF16)
    _dsa_core(iq_ref, misc_ref[...], ik_s, cq_ref, ck_s, cv_s, o_ref, *scratch,
              tq=t, n_l=n_l, ch=ch, qpos0=past, n_keys=past + t, k_sel=k_sel)


def _dsa_sample(slab2, misc, t, ikd_past, cache_k, cache_v, layer, k_sel):
    b = slab2.shape[0] // t
    past = cache_k.shape[2]
    ch = 3 * LANES
    n_l = -(-(past + t) // ch) * ch
    npair = W_HEADS // LANES
    return pl.pallas_call(
        functools.partial(_dsa_sample_kernel, t=t, past=past, n_l=n_l, ch=ch, k_sel=k_sel),
        grid=(b,),
        in_specs=[
            pl.BlockSpec((t, W_HEADS), lambda i: (i, G_IQ)),
            pl.BlockSpec((t, LANES), lambda i: (i, 0)),
            pl.BlockSpec((None, None, past, LANES), lambda i: (layer, i, 0, 0)),
            pl.BlockSpec((None, None, past, W_HEADS), lambda i: (layer, i, 0, 0)),
            pl.BlockSpec((None, None, past, W_HEADS), lambda i: (layer, i, 0, 0)),
            pl.BlockSpec((t, LANES), lambda i: (i, G_MISC * npair)),
            pl.BlockSpec((t, W_HEADS), lambda i: (i, G_CQ)),
            pl.BlockSpec((t, W_HEADS), lambda i: (i, G_CK)),
            pl.BlockSpec((t, W_HEADS), lambda i: (i, G_CV)),
        ],
        out_specs=pl.BlockSpec((t, W_HEADS), lambda i: (i, 0)),
        out_shape=jax.ShapeDtypeStruct((b * t, W_HEADS), BF16),
        scratch_shapes=[pltpu.VMEM((n_l, LANES), BF16), pltpu.VMEM((n_l, W_HEADS), BF16),
                        pltpu.VMEM((n_l, W_HEADS), BF16)] + _dsa_scratch(t, n_l),
        compiler_params=_cparams(1),
        name="dsa_sample",
    )(slab2, misc, ikd_past, cache_k, cache_v, slab2, slab2, slab2, slab2)


def _layernorm(z, g_ref, b_ref):
    mu = jnp.mean(z, axis=1, keepdims=True)
    d = z - mu
    var = jnp.mean(jnp.square(d), axis=1, keepdims=True)
    return d * lax.rsqrt(var + LN_EPS) * g_ref[...] + b_ref[...]


def _merge_kernel(x_ref, ya_ref, yb_ref, yc_ref, wgl_ref, bg_ref, wa_ref, wb_ref, wc_ref, wo_ref, g_ref, b_ref, o_ref):
    x = x_ref[...]
    xb = x.astype(BF16)
    merged = None
    for i, (y_ref, w_ref) in enumerate(((ya_ref, wa_ref), (yb_ref, wb_ref), (yc_ref, wc_ref))):
        gl = jnp.dot(xb, wgl_ref[:, i * D_MODEL:(i + 1) * D_MODEL], preferred_element_type=F32)
        gate = jax.nn.sigmoid(gl + bg_ref[i:i + 1, :])
        term = gate * jnp.dot(y_ref[...], w_ref[...], preferred_element_type=F32)
        merged = term if merged is None else merged + term
    z = DEEPNORM_ALPHA * x + jnp.dot(merged.astype(BF16), wo_ref[...], preferred_element_type=F32)
    o_ref[...] = _layernorm(z, g_ref, b_ref)


def _resident(shape, layer):
    nd = len(shape)
    return pl.BlockSpec((None,) + tuple(shape), lambda i: (layer,) + (0,) * nd, pipeline_mode=pl.Buffered(1))


def _merge(x, ya, yb, yc, wts, layer):
    m = x.shape[0]
    tm = min(TM_DENSE, m)
    row = lambda w: pl.BlockSpec((tm, w), lambda i: (i, 0))
    return pl.pallas_call(
        _merge_kernel,
        grid=(m // tm,),
        in_specs=[
            row(D_MODEL), row(W_HEADS), row(W_HEADS), row(W_HEADS),
            _resident((D_MODEL, N_BRANCH * D_MODEL), layer),
            _resident((N_BRANCH, D_MODEL), layer),
            _resident((W_HEADS, D_MODEL), layer),
            _resident((W_HEADS, D_MODEL), layer),
            _resident((W_HEADS, D_MODEL), layer),
            _resident((D_MODEL, D_MODEL), layer),
            _resident((1, D_MODEL), layer),
            _resident((1, D_MODEL), layer),
        ],
        out_specs=row(D_MODEL),
        out_shape=jax.ShapeDtypeStruct((m, D_MODEL), F32),
        compiler_params=_cparams(1),
        name="merge",
    )(x, ya, yb, yc, wts["w_gl"], wts["b_gate"], wts["w_br_a"], wts["w_br_b"], wts["w_br_c"], wts["w_o"],
      wts["ln1_g"], wts["ln1_b"])


def _ffn_kernel(x_ref, w1_ref, w3_ref, w2_ref, g_ref, b_ref, o_ref, *, d_ff, fc):
    x = x_ref[...]
    xb = x.astype(BF16)
    acc = None
    for c in range(d_ff // fc):
        sl = slice(c * fc, (c + 1) * fc)
        h1 = jnp.dot(xb, w1_ref[:, sl], preferred_element_type=F32)
        h3 = jnp.dot(xb, w3_ref[:, sl], preferred_element_type=F32)
        u = (h1 * jax.nn.sigmoid(h1) * h3).astype(BF16)
        term = jnp.dot(u, w2_ref[sl, :], preferred_element_type=F32)
        acc = term if acc is None else acc + term
    o_ref[...] = _layernorm(DEEPNORM_ALPHA * x + acc, g_ref, b_ref)


def _ffn(x, wts, layer):
    m = x.shape[0]
    tm = min(TM_DENSE, m)
    d_ff = wts["w_ff1"].shape[2]
    row = pl.BlockSpec((tm, D_MODEL), lambda i: (i, 0))
    return pl.pallas_call(
        functools.partial(_ffn_kernel, d_ff=d_ff, fc=256),
        grid=(m // tm,),
        in_specs=[
            row,
            _resident((D_MODEL, d_ff), layer),
            _resident((D_MODEL, d_ff), layer),
            _resident((d_ff, D_MODEL), layer),
            _resident((1, D_MODEL), layer),
            _resident((1, D_MODEL), layer),
        ],
        out_specs=row,
        out_shape=jax.ShapeDtypeStruct((m, D_MODEL), F32),
        compiler_params=_cparams(1),
        name="ffn",
    )(x, wts["w_ff1"], wts["w_ff3"], wts["w_ff2"], wts["ln2_g"], wts["ln2_b"])


def _rope_tables(pos, tm):
    half = HEAD_DIM // 2
    inv = ROPE_THETA ** (-jnp.arange(half, dtype=F32) / half)
    ang = pos.astype(F32)[:, None] * inv[None, :]
    cos, sin = jnp.cos(ang), jnp.sin(ang)
    cos_t = jnp.concatenate([cos, cos, cos, cos], axis=1)
    sin_t = jnp.concatenate([-sin, sin, -sin, sin], axis=1)
    reps = max(1, tm // pos.shape[0])
    return jnp.tile(cos_t, (reps, 1)), jnp.tile(sin_t, (reps, 1))


def _prep_weights(w_in, b_fgate, b_gate, diff_norm_g, w_br_a, w_br_b, w_br_c, w_o, ln1_g, ln1_b, ln2_g, ln2_b,
                  w_ff1, w_ff3, w_ff2, lam_q1, lam_k1, lam_q2, lam_k2):
    depth = w_in.shape[0]
    offs = np.cumsum((0,) + SPLIT_SIZES)
    col = lambda i: w_in[:, :, offs[i]:offs[i + 1]]
    aq, ak, av, af, bq, bk, bv, cq, ck, cv, iq, ik, iw, gl = (col(i) for i in range(len(SPLIT_SIZES)))
    qs = HEAD_DIM ** -0.5
    misc_pad = jnp.zeros((depth, D_MODEL, W_HEADS - D_IDX - H_FOX - H_IDX), w_in.dtype)
    w_slab = jnp.concatenate([aq * qs, ak, av, bq * qs, bk, bv, cq * qs, ck, cv, iq * (D_IDX ** -0.5),
                              ik, af, iw, misc_pad], axis=2).astype(BF16)
    bf_row = jnp.zeros((depth, 1, LANES), F32).at[:, 0, MISC_LOGF:MISC_LOGF + H_FOX].set(b_fgate.astype(F32))
    return {
        "w_slab": w_slab,
        "bf_row": bf_row,
        "w_gl": gl.astype(BF16),
        "b_gate": b_gate.astype(F32),
        "w_br_a": w_br_a.astype(BF16), "w_br_b": w_br_b.astype(BF16), "w_br_c": w_br_c.astype(BF16),
        "w_o": w_o.astype(BF16),
        "ln1_g": ln1_g[:, None, :].astype(F32), "ln1_b": ln1_b[:, None, :].astype(F32),
        "ln2_g": ln2_g[:, None, :].astype(F32), "ln2_b": ln2_b[:, None, :].astype(F32),
        "w_ff1": w_ff1.astype(BF16), "w_ff3": w_ff3.astype(BF16), "w_ff2": w_ff2.astype(BF16),
        "lam4": jnp.stack([lam_q1, lam_k1, lam_q2, lam_k2], axis=1).astype(F32),
        "diff_g": diff_norm_g[:, None, :].astype(F32),
    }


def _layer_group(x, wts, layer, tabs, bsz, t, caches):
    cos_t, sin_t = tabs
    proj = _inproj(x, wts["w_slab"], layer, cos_t, sin_t, wts["bf_row"], t, caches is None)
    slab, rows32, misc = proj[:3]
    logf = misc[:, MISC_LOGF:MISC_LOGF + H_FOX].reshape(bsz, t, H_FOX)
    lf_all = misc.reshape(bsz, t, LANES)
    past = 0
    if caches is not None:
        past = caches["a_logf"].shape[2]
        past_lf = jnp.pad(caches["a_logf"][layer].astype(F32),
                          ((0, 0), (0, 0), (MISC_LOGF, LANES - MISC_LOGF - H_FOX)))
        lf_all = jnp.concatenate([past_lf, lf_all], axis=1)
    n_keys = past + t
    cum = _cumsum_time(lf_all)[:, :, MISC_LOGF:MISC_LOGF + H_FOX]
    npair = H_FOX // 2
    cum_rows = cum.transpose(0, 2, 1).reshape(bsz, npair, 2, n_keys)
    cum_cols = cum.reshape(bsz, n_keys, npair, 2).transpose(0, 2, 1, 3)
    lam_init = 0.8 - 0.6 * math.exp(-0.3 * layer)
    diff_extra = (wts["lam4"][layer], wts["diff_g"][layer])
    k_sel = min(TOPK_MAX, n_keys // 4)
    if caches is None:
        slab3 = slab.reshape(bsz, t, SLAB_W)
        vt_slab = proj[3]
        ya = _pair_attn_prompt("fox", slab, slab3, vt_slab, G_AQ, G_AK, 0, (cum_rows, cum_cols), lam_init)
        yb = _pair_attn_prompt("diff", slab, slab3, vt_slab, G_BQ, G_BK, 1, diff_extra, lam_init)
        yc = _dsa_prompt(slab, slab3, vt_slab, misc, k_sel)
    else:
        cq4, ck4 = cum_cols[:, :, past:], cum_rows
        ya = _pair_attn_sample("fox", slab, t, caches["a_k"], caches["a_v"], layer, G_AQ, G_AK, G_AV,
                               (cq4, ck4), lam_init)
        yb = _pair_attn_sample("diff", slab, t, caches["b_k"], caches["b_v"], layer, G_BQ, G_BK, G_BV,
                               diff_extra, lam_init)
        yc = _dsa_sample(slab, misc, t, caches["c_ikd"], caches["c_k"], caches["c_v"], layer, k_sel)
    x = _merge(x, ya, yb, yc, wts, layer)
    x = _ffn(x, wts, layer)
    g = lambda i: rows32[:, i * W_HEADS:(i + 1) * W_HEADS]
    new_rows = (
        g(0).reshape(bsz, t, H_FOX, HEAD_DIM), g(1).reshape(bsz, t, H_FOX, HEAD_DIM), logf,
        g(2).reshape(bsz, t, H_DIFF, 2, HEAD_DIM), g(3).reshape(bsz, t, H_DIFF, 2 * HEAD_DIM),
        g(4).reshape(bsz, t, H_DSA, HEAD_DIM), g(5).reshape(bsz, t, H_DSA, HEAD_DIM),
        misc[:, :D_IDX].reshape(bsz, t, D_IDX),
    )
    return x, new_rows


def kernel(x_prompt, x_sample, cache_a_k, cache_a_v, cache_a_logf, cache_b_k, cache_b_v, cache_c_k, cache_c_v, cache_c_idx, w_in, b_fgate, b_gate, lam_q1, lam_k1, lam_q2, lam_k2, diff_norm_g, w_br_a, w_br_b, w_br_c, w_o, ln1_g, ln1_b, ln2_g, ln2_b, w_ff1, w_ff3, w_ff2):
    depth = w_in.shape[0]
    bp, tp, _ = x_prompt.shape
    bs, ts, _ = x_sample.shape
    past = cache_a_k.shape[2]
    wts = _prep_weights(w_in, b_fgate, b_gate, diff_norm_g, w_br_a, w_br_b, w_br_c, w_o, ln1_g, ln1_b, ln2_g, ln2_b,
                        w_ff1, w_ff3, w_ff2, lam_q1, lam_k1, lam_q2, lam_k2)
    tabs_p = _rope_tables(jnp.arange(tp, dtype=I32), min(TM_DENSE, bp * tp))
    tabs_s = _rope_tables(past + jnp.arange(ts, dtype=I32), min(TM_DENSE, bs * ts))
    flat = lambda c: c.reshape(c.shape[0], c.shape[1], c.shape[2], -1)
    caches = {
        "a_k": flat(cache_a_k), "a_v": flat(cache_a_v), "a_logf": cache_a_logf,
        "b_k": flat(cache_b_k), "b_v": flat(cache_b_v),
        "c_k": flat(cache_c_k), "c_v": flat(cache_c_v),
        "c_ikd": jnp.concatenate([cache_c_idx, cache_c_idx], axis=-1).astype(BF16),
    }
    yp = x_prompt.reshape(bp * tp, D_MODEL)
    ys = x_sample.reshape(bs * ts, D_MODEL)
    rows_p, rows_s = [], []
    for layer in range(depth):
        yp, rp = _layer_group(yp, wts, layer, tabs_p, bp, tp, None)
        ys, rs = _layer_group(ys, wts, layer, tabs_s, bs, ts, caches)
        rows_p.append(rp)
        rows_s.append(rs)
    outs_p = [jnp.stack([r[i] for r in rows_p]) for i in range(8)]
    outs_s = [jnp.stack([r[i] for r in rows_s]) for i in range(8)]
    return (yp.reshape(bp, tp, D_MODEL), ys.reshape(bs, ts, D_MODEL), *outs_p, *outs_s)
```

```python
import functools
import math

import jax
import jax.numpy as jnp
import numpy as np
from jax import lax
from jax.experimental import pallas as pl
from jax.experimental.pallas import tpu as pltpu

F32 = jnp.float32
BF16 = jnp.bfloat16
I32 = jnp.int32

D_MODEL = 1024
HEAD_DIM = 64
H_FOX = 8
H_DIFF = 4
H_DSA = 8
H_IDX = 8
D_IDX = 64
CHUNK = 64
CHUNK_SHIFT = 6
TOPK_MAX = 256
ROPE_THETA = 10000.0
N_BRANCH = 3
LN_EPS = 1e-5
MODEL_DEPTH = 4
DEEPNORM_ALPHA = (2 * MODEL_DEPTH) ** 0.25
LOG2E = math.log2(math.e)
W_HEADS = 512
SPLIT_SIZES = (W_HEADS, W_HEADS, W_HEADS, H_FOX, W_HEADS, W_HEADS, W_HEADS, W_HEADS, W_HEADS, W_HEADS,
               H_IDX * D_IDX, D_IDX, H_IDX, N_BRANCH * D_MODEL)

LANES = 128
N_GROUPS = 11
SLAB_W = N_GROUPS * W_HEADS
G_AQ, G_AK, G_AV, G_BQ, G_BK, G_BV, G_CQ, G_CK, G_CV, G_IQ, G_MISC = range(N_GROUPS)
MISC_LOGF = 64
MISC_IW = 72
VMEM_LIMIT = 56 * 1024 * 1024
NEG_INF = float("-inf")
INT_MIN = np.int32(-2 ** 31)
INT_MAX_MASK = np.int32(2 ** 31 - 1)

TM_DENSE = 512
TQ_ATTN = 512
TQ_DSA = 512
SEG_DSA = 512
CH_DSA = 512


def _cparams(n_axes):
    return pltpu.CompilerParams(dimension_semantics=("arbitrary",) * n_axes, vmem_limit_bytes=VMEM_LIMIT)


def _lane_iota(n=LANES):
    return lax.broadcasted_iota(I32, (1, n), 1)


def _inproj_kernel(x_ref, w_ref, cos_ref, sin_ref, bf_ref, s16_ref, misc_ref, *rest, feature_major):
    if feature_major:
        vt_ref, akt_ref, avt_ref, bkt_ref, bv_ref, ckt_ref, cvt_ref, misct_ref, xb_ref = rest
        t_out = {G_AK: akt_ref, G_AV: avt_ref, G_BK: bkt_ref, G_CK: ckt_ref, G_CV: cvt_ref}
        r_out = {G_BV: bv_ref}
    else:
        ak_ref, av_ref, bk_ref, bv_ref, ck_ref, cv_ref, xb_ref = rest
        t_out = {}
        r_out = {G_AK: ak_ref, G_AV: av_ref, G_BK: bk_ref, G_BV: bv_ref, G_CK: ck_ref, G_CV: cv_ref}
    j = pl.program_id(1)

    @pl.when(j == 0)
    def _():
        xb_ref[...] = x_ref[...].astype(BF16)

    h = jnp.dot(xb_ref[...], w_ref[...], preferred_element_type=F32)
    lane = _lane_iota()
    sel_up = (lane & (HEAD_DIM - 1)) < HEAD_DIM // 2

    def rope128(xb):
        up = pltpu.roll(xb, LANES - HEAD_DIM // 2, 1)
        dn = pltpu.roll(xb, HEAD_DIM // 2, 1)
        return xb * cos_ref[...] + jnp.where(sel_up, up, dn) * sin_ref[...]

    is_rope = (j == G_BQ) | (j == G_BK) | (j == G_CQ) | (j == G_CK) | (j == G_IQ)
    is_v = (j == G_AV) | (j == G_BV) | (j == G_CV)

    def emit(val):
        s16_ref[...] = val.astype(BF16)
        for g, ref in r_out.items():
            @pl.when(j == g)
            def _(ref=ref):
                ref[...] = val
        if feature_major:
            @pl.when(is_v | (j == G_AK) | (j == G_BK) | (j == G_CK))
            def _():
                val_t = val.T
                for g, ref in t_out.items():
                    @pl.when(j == g)
                    def _(ref=ref):
                        ref[0] = val_t

                @pl.when(is_v)
                def _():
                    vt_ref[0] = val_t.astype(BF16)

    @pl.when(is_rope)
    def _():
        emit(jnp.concatenate([rope128(h[:, c * LANES:(c + 1) * LANES]) for c in range(W_HEADS // LANES)], axis=1))

    @pl.when(jnp.logical_not(is_rope) & (j < G_MISC))
    def _():
        emit(h)

    @pl.when(j == G_MISC)
    def _():
        hb = h[:, :LANES]
        roped = rope128(hb)
        z = hb + bf_ref[...]
        logf = jnp.minimum(z, 0.0) - jnp.log1p(jnp.exp(-jnp.abs(z)))
        misc = jnp.where(lane < MISC_LOGF, roped, jnp.where(lane < MISC_IW, logf, hb))
        misc_ref[...] = misc
        if feature_major:
            misct_ref[0] = misc.T
        ikd = jnp.where(lane < D_IDX, roped, pltpu.roll(roped, D_IDX, 1))
        s16_ref[:, :LANES] = ikd.astype(BF16)
        s16_ref[:, LANES:] = jnp.zeros((s16_ref.shape[0], W_HEADS - LANES), BF16)


def _inproj(x, w_all, layer, cos_t, sin_t, bf_row, seq, feature_major):
    m = x.shape[0]
    tm = min(TM_DENSE, m)
    n_tab = cos_t.shape[0] // tm
    tab_map = lambda i, j: (i % n_tab, 0)
    rows32 = (pl.BlockSpec((tm, W_HEADS), lambda i, j: (i, 0)), jax.ShapeDtypeStruct((m, W_HEADS), F32))
    out_specs = [pl.BlockSpec((tm, W_HEADS), lambda i, j: (i, j)), pl.BlockSpec((tm, LANES), lambda i, j: (i, 0))]
    out_shape = [jax.ShapeDtypeStruct((m, SLAB_W), BF16), jax.ShapeDtypeStruct((m, LANES), F32)]
    if feature_major:
        nt = seq // tm
        bsz = m // seq
        cols32 = (pl.BlockSpec((1, W_HEADS, tm), lambda i, j: (i // nt, 0, i % nt)),
                  jax.ShapeDtypeStruct((bsz, W_HEADS, seq), F32))
        outs = [(pl.BlockSpec((1, W_HEADS, tm), lambda i, j: (i // nt, jnp.minimum(j // 3, 2), i % nt)),
                 jax.ShapeDtypeStruct((bsz, 3 * W_HEADS, seq), BF16)),
                cols32, cols32, cols32, rows32, cols32, cols32,
                (pl.BlockSpec((1, LANES, tm), lambda i, j: (i // nt, 0, i % nt)),
                 jax.ShapeDtypeStruct((bsz, LANES, seq), F32))]
    else:
        outs = [rows32] * 6
    out_specs += [o[0] for o in outs]
    out_shape += [o[1] for o in outs]
    return pl.pallas_call(
        functools.partial(_inproj_kernel, feature_major=feature_major),
        grid=(m // tm, N_GROUPS),
        in_specs=[
            pl.BlockSpec((tm, D_MODEL), lambda i, j: (i, 0)),
            pl.BlockSpec((None, D_MODEL, W_HEADS), lambda i, j: (layer, 0, j)),
            pl.BlockSpec((tm, LANES), tab_map),
            pl.BlockSpec((tm, LANES), tab_map),
            pl.BlockSpec((None, 1, LANES), lambda i, j: (layer, 0, 0)),
        ],
        out_specs=out_specs,
        out_shape=out_shape,
        scratch_shapes=[pltpu.VMEM((tm, D_MODEL), BF16)],
        compiler_params=_cparams(2),
        name="inproj",
    )(x, w_all, cos_t, sin_t, bf_row)


def _split3(x):
    hi = x.astype(BF16)
    r1 = x - hi.astype(F32)
    mid = r1.astype(BF16)
    lo = (r1 - mid.astype(F32)).astype(BF16)
    return hi, mid, lo


def _cumsum_kernel(x_ref, o_ref, carry_ref):
    t = pl.program_id(1)

    @pl.when(t == 0)
    def _():
        carry_ref[...] = jnp.zeros_like(carry_ref)

    x = x_ref[0]
    tl = x.shape[0]
    r = lax.broadcasted_iota(I32, (tl, tl), 0)
    c = lax.broadcasted_iota(I32, (tl, tl), 1)
    tri = jnp.where(c <= r, 1.0, 0.0).astype(BF16)
    hi, mid, lo = _split3(x)
    dot = lambda a: jnp.dot(tri, a, preferred_element_type=F32)
    cs = (dot(lo) + dot(mid)) + dot(hi) + carry_ref[...]
    o_ref[0] = cs * LOG2E
    carry_ref[...] = cs[tl - 1:tl, :]


def _cumsum_time(x):
    b, l, h = x.shape
    tl = 512 if l % 512 == 0 else l
    return pl.pallas_call(
        _cumsum_kernel,
        grid=(b, l // tl),
        in_specs=[pl.BlockSpec((1, tl, h), lambda i, t: (i, t, 0))],
        out_specs=pl.BlockSpec((1, tl, h), lambda i, t: (i, t, 0)),
        out_shape=jax.ShapeDtypeStruct((b, l, h), F32),
        scratch_shapes=[pltpu.VMEM((1, h), F32)],
        compiler_params=_cparams(2),
        name="cumsum",
    )(x)


def _split_pair(q):
    lane = _lane_iota()
    zero = jnp.zeros_like(q)
    return jnp.where(lane < HEAD_DIM, q, zero), jnp.where(lane >= HEAD_DIM, q, zero)


def _qk(q, k):
    return lax.dot_general(q, k, (((1,), (1,)), ((), ())), preferred_element_type=F32)


def _online_update(s, v, m_old, l_old, acc_old):
    m_new = jnp.maximum(m_old, jnp.max(s, axis=1, keepdims=True))
    m_safe = jnp.where(m_new == NEG_INF, 0.0, m_new)
    alpha = jnp.exp2(m_old - m_safe)
    p = jnp.exp2(s - m_safe)
    l_new = alpha * l_old + jnp.sum(p, axis=1, keepdims=True)
    acc_new = alpha * acc_old + jnp.dot(p.astype(BF16), v, preferred_element_type=F32)
    return m_new, l_new, acc_new


def _causal_mask(kind, rows, cols, row0, col0):
    r = row0 + lax.broadcasted_iota(I32, (rows, 1), 0)
    c = col0 + lax.broadcasted_iota(I32, (1, cols), 1)
    if kind == "fox":
        return c <= r
    return (c >> CHUNK_SHIFT) <= (r >> CHUNK_SHIFT)


def _diff_lambda(lam_ref, lam_init):
    lp = lam_ref[...]
    s1 = jnp.sum(lp[0:1] * lp[1:2], axis=1, keepdims=True)
    s2 = jnp.sum(lp[2:3] * lp[3:4], axis=1, keepdims=True)
    return jnp.exp(s1) - jnp.exp(s2) + lam_init


def _pair_epilogue(kind, o0, o1, lam_ref, g_ref, lam_init):
    if kind == "fox":
        return jnp.where(_lane_iota() < HEAD_DIM, o0, o1)
    o = o0 - _diff_lambda(lam_ref, lam_init) * o1
    o = o * lax.rsqrt(jnp.mean(jnp.square(o), axis=1, keepdims=True) + LN_EPS) * g_ref[...]
    return o * (1.0 - lam_init)


SUM_ROWS = 16


def _with_sum_rows(vt):
    return jnp.concatenate([vt, jnp.ones((SUM_ROWS, vt.shape[1]), vt.dtype)], axis=0)


def _online_update_t(st, vt_aug, m_old, acc_old):
    m_new = jnp.maximum(m_old, jnp.max(st, axis=0, keepdims=True))
    m_safe = jnp.where(m_new == NEG_INF, 0.0, m_new)
    alpha = jnp.exp2(m_old - m_safe)
    pt = jnp.exp2(st - m_safe).astype(BF16)
    acc_new = alpha * acc_old + jnp.dot(vt_aug, pt, preferred_element_type=F32)
    return m_new, acc_new


def _normalised(acc):
    return acc[:LANES] * (1.0 / acc[LANES:LANES + 1])


def _causal_mask_t(kind, keys, queries):
    kp = lax.broadcasted_iota(I32, (keys, 1), 0)
    qp = lax.broadcasted_iota(I32, (1, queries), 1)
    if kind == "fox":
        return kp <= qp
    return (kp >> CHUNK_SHIFT) <= (qp >> CHUNK_SHIFT)


def _init_stats(m_ref, acc_ref):
    m_ref[...] = jnp.full(m_ref.shape, NEG_INF, F32)
    acc_ref[...] = jnp.zeros(acc_ref.shape, F32)


def _pair_attn_prompt_kernel(*refs, kind, tq, lam_init):
    if kind == "fox":
        q_ref, k_ref, vt_ref, cq_ref, ck_ref, o_ref, m_ref, acc_ref, st_ref = refs
        lam_ref = g_ref = None
    else:
        q_ref, k_ref, vt_ref, lam_ref, g_ref, o_ref, m_ref, acc_ref, st_ref = refs
    qi = pl.program_id(2)
    qs = _split_pair(q_ref[...])
    _init_stats(m_ref, acc_ref)

    def scores(j, slot):
        start = pl.multiple_of(j * tq, tq)
        k = k_ref[0, pl.ds(start, tq), :]
        for h in range(2):
            st = _qk(k, qs[h])
            if kind == "fox":
                st = st + cq_ref[0, 0, h:h + 1, :] - ck_ref[0, 0, pl.ds(start, tq), h:h + 1]
            st_ref[slot, h] = st

    def consume(j, slot, masked):
        start = pl.multiple_of(j * tq, tq)
        vt = _with_sum_rows(vt_ref[0, :, pl.ds(start, tq)])
        for h in range(2):
            st = st_ref[slot, h]
            if masked:
                st = jnp.where(_causal_mask_t(kind, tq, tq), st, NEG_INF)
            m_ref[h], acc_ref[h] = _online_update_t(st, vt, m_ref[h], acc_ref[h])

    scores(0, 0)

    def pair(jj, carry):
        j = 2 * jj
        consume(j, 0, False)
        scores(j + 1, 1)
        consume(j + 1, 1, False)
        scores(j + 2, 0)
        return carry

    lax.fori_loop(0, qi // 2, pair, 0)

    @pl.when(qi % 2 == 1)
    def _():
        consume(qi - 1, 0, False)
        scores(qi, 1)
        consume(qi, 1, True)

    @pl.when(qi % 2 == 0)
    def _():
        consume(qi, 0, True)

    o0 = _normalised(acc_ref[0]).T
    o1 = _normalised(acc_ref[1]).T
    o_ref[...] = _pair_epilogue(kind, o0, o1, lam_ref, g_ref, lam_init).astype(BF16)


def _pair_attn_prompt(kind, slab2, slab3, vt_slab, qg, kg, vt_group, extra, lam_init):
    b, t, _ = slab3.shape
    tq = min(TQ_ATTN, t)
    nq = t // tq
    npair = W_HEADS // LANES
    in_specs = [
        pl.BlockSpec((tq, LANES), lambda i, p, q: (i * nq + q, qg * npair + p)),
        pl.BlockSpec((1, t, LANES), lambda i, p, q: (i, 0, kg * npair + p)),
        pl.BlockSpec((1, LANES, t), lambda i, p, q: (i, vt_group * npair + p, 0)),
    ]
    if kind == "fox":
        in_specs += [
            pl.BlockSpec((1, 1, 2, tq), lambda i, p, q: (i, p, 0, q)),
            pl.BlockSpec((1, 1, t, 2), lambda i, p, q: (i, p, 0, 0)),
        ]
    else:
        in_specs += [
            pl.BlockSpec((4, HEAD_DIM), lambda i, p, q: (0, 0)),
            pl.BlockSpec((1, LANES), lambda i, p, q: (0, 0)),
        ]
    return pl.pallas_call(
        functools.partial(_pair_attn_prompt_kernel, kind=kind, tq=tq, lam_init=lam_init),
        grid=(b, npair, nq),
        in_specs=in_specs,
        out_specs=pl.BlockSpec((tq, LANES), lambda i, p, q: (i * nq + q, p)),
        out_shape=jax.ShapeDtypeStruct((b * t, W_HEADS), BF16),
        scratch_shapes=[pltpu.VMEM((2, 1, tq), F32), pltpu.VMEM((2, LANES + SUM_ROWS, tq), F32),
                        pltpu.VMEM((2, 2, tq, tq), F32)],
        compiler_params=_cparams(3),
        name=kind + "_prompt",
    )(slab2, slab3, vt_slab, *extra)


def _pair_attn_sample_kernel(*refs, kind, t, past, lam_init):
    if kind == "fox":
        q_ref, kp_ref, vp_ref, kn_ref, vn_ref, cq_ref, ck_ref, o_ref = refs
        lam_ref = g_ref = None
    else:
        q_ref, kp_ref, vp_ref, kn_ref, vn_ref, lam_ref, g_ref, o_ref = refs
    qs = _split_pair(q_ref[...])
    kp = kp_ref[...].T.astype(BF16)
    vp = (vp_ref[...].T if kind == "fox" else vp_ref[...]).astype(BF16)
    kn = kn_ref[...]
    vn = vn_ref[...]
    outs = []
    for h in range(2):
        m = jnp.full((t, 1), NEG_INF, F32)
        l = jnp.zeros((t, 1), F32)
        acc = jnp.zeros((t, LANES), F32)
        s = _qk(qs[h], kp)
        if kind == "fox":
            s = s + cq_ref[0, 0, :, h:h + 1] - ck_ref[0, 0, h:h + 1, :past]
        m, l, acc = _online_update(s, vp, m, l, acc)
        s = _qk(qs[h], kn)
        if kind == "fox":
            s = s + cq_ref[0, 0, :, h:h + 1] - ck_ref[0, 0, h:h + 1, past:past + t]
        s = jnp.where(_causal_mask(kind, t, t, past, past), s, NEG_INF)
        m, l, acc = _online_update(s, vn, m, l, acc)
        outs.append(acc * (1.0 / l))
    o_ref[...] = _pair_epilogue(kind, outs[0], outs[1], lam_ref, g_ref, lam_init).astype(BF16)


def _pair_attn_sample(kind, slab2, t, cache_k, cache_v, layer, qg, kg, vg, extra, lam_init):
    b = slab2.shape[0] // t
    past = cache_k.shape[3]
    npair = W_HEADS // LANES
    feature_major = pl.BlockSpec((None, None, LANES, past), lambda i, p: (layer, i, p, 0))
    time_major = pl.BlockSpec((None, None, past, LANES), lambda i, p: (layer, i, 0, p))
    in_specs = [
        pl.BlockSpec((t, LANES), lambda i, p: (i, qg * npair + p)),
        feature_major,
        feature_major if kind == "fox" else time_major,
        pl.BlockSpec((t, LANES), lambda i, p: (i, kg * npair + p)),
        pl.BlockSpec((t, LANES), lambda i, p: (i, vg * npair + p)),
    ]
    if kind == "fox":
        in_specs += [
            pl.BlockSpec((1, 1, t, 2), lambda i, p: (i, p, 0, 0)),
            pl.BlockSpec((1, 1, 2, past + t), lambda i, p: (i, p, 0, 0)),
        ]
    else:
        in_specs += [
            pl.BlockSpec((4, HEAD_DIM), lambda i, p: (0, 0)),
            pl.BlockSpec((1, LANES), lambda i, p: (0, 0)),
        ]
    return pl.pallas_call(
        functools.partial(_pair_attn_sample_kernel, kind=kind, t=t, past=past, lam_init=lam_init),
        grid=(b, npair),
        in_specs=in_specs,
        out_specs=pl.BlockSpec((t, LANES), lambda i, p: (i, p)),
        out_shape=jax.ShapeDtypeStruct((b * t, W_HEADS), BF16),
        compiler_params=_cparams(2),
        name=kind + "_sample",
    )(slab2, cache_k, cache_v, slab2, slab2, *extra)


def _dsa_core(iq_ref, iw, ikd_ref, cq_ref, ck_ref, cv_ref, o_ref, keys_ref, bias_ref, m_ref, l_ref, acc_ref,
              *, tq, n_l, ch, qpos0, n_keys, k_sel):
    nch = n_l // ch
    lane = _lane_iota()
    qchunk = (qpos0 + lax.broadcasted_iota(I32, (tq, 1), 0)) >> CHUNK_SHIFT
    w = [iw[:, MISC_IW + h:MISC_IW + h + 1] for h in range(H_IDX)]

    def score_chunk(c, carry):
        start = pl.multiple_of(c * ch, ch)
        ik = ikd_ref[pl.ds(start, ch), :]
        acc = jnp.zeros((tq, ch), F32)
        for hp in range(H_IDX // 2):
            pair = _split_pair(iq_ref[:, hp * LANES:(hp + 1) * LANES])
            for half in range(2):
                acc = acc + jnp.maximum(_qk(pair[half], ik), 0.0) * w[2 * hp + half]
        score = acc * (H_IDX ** -0.5) + 0.0
        bits = lax.bitcast_convert_type(score, I32)
        key = bits ^ ((bits >> 31) & INT_MAX_MASK)
        kpos = start + lax.broadcasted_iota(I32, (1, ch), 1)
        adm = ((kpos >> CHUNK_SHIFT) <= qchunk) & (kpos < n_keys)
        keys_ref[:, pl.ds(start, ch)] = jnp.where(adm, key, INT_MIN)
        return carry

    lax.fori_loop(0, nch, score_chunk, 0)

    def count(pred):
        def body(c, a):
            start = pl.multiple_of(c * ch, ch)
            hit = jnp.where(pred(keys_ref[:, pl.ds(start, ch)]), 1.0, 0.0)
            for t in range(ch // LANES):
                a = a + hit[:, t * LANES:(t + 1) * LANES]
            return a
        part = lax.fori_loop(0, nch, body, jnp.zeros((tq, LANES), F32))
        return jnp.sum(part, axis=1, keepdims=True)

    def bit_step(i, thr_u):
        cand_u = thr_u | lax.shift_left(jnp.int32(1), 31 - i)
        cand_s = cand_u ^ INT_MIN
        n_ge = count(lambda kk: kk >= cand_s)
        return jnp.where(n_ge >= k_sel, cand_u, thr_u)

    thr = lax.fori_loop(0, 32, bit_step, jnp.zeros((tq, 1), I32)) ^ INT_MIN
    need = k_sel - count(lambda kk: kk > thr)

    r128 = lax.broadcasted_iota(I32, (LANES, LANES), 0)
    c128 = lax.broadcasted_iota(I32, (LANES, LANES), 1)
    before = jnp.where(r128 < c128, 1.0, 0.0).astype(BF16)

    def select_block(t, seen):
        start = pl.multiple_of(t * LANES, LANES)
        kk = keys_ref[:, pl.ds(start, LANES)]
        eq = kk == thr
        eqf = jnp.where(eq, 1.0, 0.0)
        rank = seen + jnp.dot(eqf.astype(BF16), before, preferred_element_type=F32)
        sel = ((kk > thr) | (eq & (rank < need))) & (kk != INT_MIN)
        bias_ref[:, pl.ds(start, LANES)] = jnp.where(sel, 0.0, NEG_INF)
        return seen + jnp.sum(eqf, axis=1, keepdims=True)

    lax.fori_loop(0, n_l // LANES, select_block, jnp.zeros((tq, 1), F32))

    for p in range(H_DSA // 2):
        sl = slice(p * LANES, (p + 1) * LANES)
        qs = _split_pair(cq_ref[:, sl])
        m_ref[...] = jnp.full(m_ref.shape, NEG_INF, F32)
        l_ref[...] = jnp.zeros(l_ref.shape, F32)
        acc_ref[...] = jnp.zeros(acc_ref.shape, F32)

        def attend(c, carry):
            start = pl.multiple_of(c * ch, ch)
            k = ck_ref[pl.ds(start, ch), sl]
            v = cv_ref[pl.ds(start, ch), sl]
            bias = bias_ref[:, pl.ds(start, ch)]
            for h in range(2):
                s = _qk(qs[h], k) + bias
                m_ref[h], l_ref[h], acc_ref[h] = _online_update(s, v, m_ref[h], l_ref[h], acc_ref[h])
            return carry

        lax.fori_loop(0, nch, attend, 0)
        o0 = acc_ref[0] * (1.0 / l_ref[0])
        o1 = acc_ref[1] * (1.0 / l_ref[1])
        o_ref[:, sl] = jnp.where(lane < HEAD_DIM, o0, o1).astype(BF16)


def _dsa_scratch(tq, n_l):
    return [pltpu.VMEM((tq, n_l), I32), pltpu.VMEM((tq, n_l), F32),
            pltpu.VMEM((2, tq, 1), F32), pltpu.VMEM((2, tq, 1), F32), pltpu.VMEM((2, tq, LANES), F32)]


def _dsa_core_t(iq_ref, iw_t, ikd_ref, cq_ref, ck_ref, cvt_ref, o_ref, keys_ref, bias_ref, m_ref, acc_ref, st_ref,
                *, tq, n_l, ch, qpos0, n_keys, k_sel):
    nch = n_l // ch
    qchunk = (qpos0 + _lane_iota(tq)) >> CHUNK_SHIFT
    w = [iw_t[MISC_IW + h:MISC_IW + h + 1, :] for h in range(H_IDX)]
    iqs = []
    for hp in range(H_IDX // 2):
        iqs.extend(_split_pair(iq_ref[:, hp * LANES:(hp + 1) * LANES]))

    def score_chunk(c, carry):
        start = pl.multiple_of(c * ch, ch)
        ik = ikd_ref[pl.ds(start, ch), :]
        acc = jnp.zeros((ch, tq), F32)
        for h in range(H_IDX):
            acc = acc + jnp.maximum(_qk(ik, iqs[h]), 0.0) * w[h]
        score = acc * (H_IDX ** -0.5) + 0.0
        bits = lax.bitcast_convert_type(score, I32)
        key = bits ^ ((bits >> 31) & INT_MAX_MASK)
        kpos = start + lax.broadcasted_iota(I32, (ch, 1), 0)
        adm = ((kpos >> CHUNK_SHIFT) <= qchunk) & (kpos < n_keys)
        keys_ref[pl.ds(start, ch), :] = jnp.where(adm, key, INT_MIN)
        return carry

    lax.fori_loop(0, nch, score_chunk, 0)

    n_acc = 4

    def count(pred):
        def body(c, accs):
            start = pl.multiple_of(c * ch, ch)
            kk = keys_ref[pl.ds(start, ch), :]
            accs = list(accs)
            for g in range(ch // 8):
                a = accs[g % n_acc]
                accs[g % n_acc] = jnp.where(pred(kk[g * 8:(g + 1) * 8, :]), a + 1.0, a)
            return tuple(accs)
        accs = lax.fori_loop(0, nch, body, tuple(jnp.zeros((8, tq), F32) for _ in range(n_acc)))
        return jnp.sum((accs[0] + accs[1]) + (accs[2] + accs[3]), axis=0, keepdims=True)

    few = count(lambda kk: kk != INT_MIN) <= k_sel

    def search_on(state):
        i, _, n_thr = state
        open_rows = jnp.where((n_thr == k_sel) | few, 0, 1)
        return (i < 32) & (jnp.max(open_rows) > 0)

    def bit_step(state):
        i, thr_u, n_thr = state
        cand_u = thr_u | lax.shift_left(jnp.int32(1), 31 - i)
        cand_s = cand_u ^ INT_MIN
        n_ge = count(lambda kk: kk >= cand_s)
        take = n_ge >= k_sel
        return i + 1, jnp.where(take, cand_u, thr_u), jnp.where(take, n_ge, n_thr)

    _, thr_u, _ = lax.while_loop(search_on, bit_step,
                                 (jnp.int32(0), jnp.zeros((1, tq), I32), jnp.full((1, tq), float(n_l), F32)))
    thr = thr_u ^ INT_MIN
    need = k_sel - count(lambda kk: kk > thr)

    sb = 2 * LANES
    r_i = lax.broadcasted_iota(I32, (sb, sb), 0)
    c_i = lax.broadcasted_iota(I32, (sb, sb), 1)
    before = jnp.where(c_i < r_i, 1.0, 0.0).astype(BF16)

    def select_block(t, seen):
        start = pl.multiple_of(t * sb, sb)
        kk = keys_ref[pl.ds(start, sb), :]
        eq = kk == thr
        eqf = jnp.where(eq, 1.0, 0.0)
        rank = seen + jnp.dot(before, eqf.astype(BF16), preferred_element_type=F32)
        sel = ((kk > thr) | (eq & (rank < need))) & (kk != INT_MIN)
        bias_ref[pl.ds(start, sb), :] = jnp.where(sel, 0.0, NEG_INF)
        return seen + jnp.sum(eqf, axis=0, keepdims=True)

    lax.fori_loop(0, n_l // sb, select_block, jnp.zeros((1, tq), F32))

    head0_rows = lax.broadcasted_iota(I32, (LANES, 1), 0) < HEAD_DIM
    for p in range(H_DSA // 2):
        sl = slice(p * LANES, (p + 1) * LANES)
        qs = _split_pair(cq_ref[:, sl])
        _init_stats(m_ref, acc_ref)

        def scores(c, slot):
            start = pl.multiple_of(c * ch, ch)
            k = ck_ref[pl.ds(start, ch), sl]
            bias = bias_ref[pl.ds(start, ch), :]
            for h in range(2):
                st_ref[slot, h] = _qk(k, qs[h]) + bias

        def consume(c, slot):
            start = pl.multiple_of(c * ch, ch)
            vt = _with_sum_rows(cvt_ref[sl, pl.ds(start, ch)])
            for h in range(2):
                m_ref[h], acc_ref[h] = _online_update_t(st_ref[slot, h], vt, m_ref[h], acc_ref[h])

        scores(0, 0)

        def attend_pair(cc, carry):
            c = 2 * cc
            consume(c, 0)
            scores(c + 1, 1)
            consume(c + 1, 1)
            scores(c + 2, 0)
            return carry

        lax.fori_loop(0, (nch - 1) // 2, attend_pair, 0)
        if nch % 2 == 0:
            consume(nch - 2, 0)
            scores(nch - 1, 1)
            consume(nch - 1, 1)
        else:
            consume(nch - 1, 0)
        ot = jnp.where(head0_rows, _normalised(acc_ref[0]), _normalised(acc_ref[1]))
        o_ref[:, sl] = ot.T.astype(BF16)


def _dsa_prompt_kernel(iq_ref, misc_ref, ikd_ref, cq_ref, ck_ref, cvt_ref, o_ref, *scratch, tq, n_l, q0, n_keys, k_sel):
    qpos0 = q0 + pl.program_id(1) * tq
    _dsa_core_t(iq_ref, misc_ref[...].T, ikd_ref.at[0], cq_ref, ck_ref.at[0], cvt_ref.at[0], o_ref, *scratch,
                tq=tq, n_l=n_l, ch=min(CH_DSA, n_l), qpos0=qpos0, n_keys=n_keys, k_sel=k_sel)


def _dsa_prompt_segment(slab2, slab3, vt_slab, misc, seg, seg_len, k_sel):
    b, t, _ = slab3.shape
    tq = min(TQ_DSA, seg_len)
    nqs = seg_len // tq
    nq = t // tq
    n_l = (seg + 1) * seg_len
    row = lambda i, q: i * nq + seg * nqs + q
    npair = W_HEADS // LANES
    return pl.pallas_call(
        functools.partial(_dsa_prompt_kernel, tq=tq, n_l=n_l, q0=seg * seg_len, n_keys=t, k_sel=k_sel),
        grid=(b, nqs),
        in_specs=[
            pl.BlockSpec((tq, W_HEADS), lambda i, q: (row(i, q), G_IQ)),
            pl.BlockSpec((tq, LANES), lambda i, q: (row(i, q), 0)),
            pl.BlockSpec((1, n_l, LANES), lambda i, q: (i, 0, G_MISC * npair)),
            pl.BlockSpec((tq, W_HEADS), lambda i, q: (row(i, q), G_CQ)),
            pl.BlockSpec((1, n_l, W_HEADS), lambda i, q: (i, 0, G_CK)),
            pl.BlockSpec((1, W_HEADS, n_l), lambda i, q: (i, 2, 0)),
        ],
        out_specs=pl.BlockSpec((tq, W_HEADS), lambda i, q: (i * nqs + q, 0)),
        out_shape=jax.ShapeDtypeStruct((b * seg_len, W_HEADS), BF16),
        scratch_shapes=[pltpu.VMEM((n_l, tq), I32), pltpu.VMEM((n_l, tq), F32),
                        pltpu.VMEM((2, 1, tq), F32), pltpu.VMEM((2, LANES + SUM_ROWS, tq), F32),
                        pltpu.VMEM((2, 2, min(CH_DSA, n_l), tq), F32)],
        compiler_params=_cparams(2),
        name="dsa_prompt",
    )(slab2, misc, slab3, slab2, slab3, vt_slab)


def _dsa_prompt(slab2, slab3, vt_slab, misc, k_sel):
    b, t, _ = slab3.shape
    seg_len = min(SEG_DSA, t)
    segs = [_dsa_prompt_segment(slab2, slab3, vt_slab, misc, s, seg_len, k_sel) for s in range(t // seg_len)]
    y = jnp.stack([s.reshape(b, seg_len, W_HEADS) for s in segs], axis=1)
    return y.reshape(b * t, W_HEADS)


def _dsa_sample_kernel(iq_ref, misc_ref, ikp_ref, ckp_ref, cvp_ref, ikn_ref, cq_ref, ckn_ref, cvn_ref, o_ref,
                       ik_s, ck_s, cv_s, *scratch, t, past, n_l, ch, k_sel):
    pad = n_l - past - t
    ikp = ikp_ref[...]
    ik_s[:past, :] = jnp.concatenate([ikp, ikp], axis=0).T.astype(BF16)
    ik_s[past:past + t, :] = ikn_ref[...]
    ik_s[past + t:, :] = jnp.zeros((pad, LANES), BF16)
    ck_s[:past, :] = ckp_ref[...].T.astype(BF16)
    ck_s[past:past + t, :] = ckn_ref[...]
    ck_s[past + t:, :] = jnp.zeros((pad, W_HEADS), BF16)
    cv_s[:past, :] = cvp_ref[...].T.astype(BF16)
    cv_s[past:past + t, :] = cvn_ref[...]
    cv_s[past + t:, :] = jnp.zeros((pad, W_HEADS), BF16)
    _dsa_core(iq_ref, misc_ref[...], ik_s, cq_ref, ck_s, cv_s, o_ref, *scratch,
              tq=t, n_l=n_l, ch=ch, qpos0=past, n_keys=past + t, k_sel=k_sel)


def _dsa_sample(slab2, misc, t, ikd_past, cache_k, cache_v, layer, k_sel):
    b = slab2.shape[0] // t
    past = cache_k.shape[3]
    ch = 3 * LANES
    n_l = -(-(past + t) // ch) * ch
    npair = W_HEADS // LANES
    return pl.pallas_call(
        functools.partial(_dsa_sample_kernel, t=t, past=past, n_l=n_l, ch=ch, k_sel=k_sel),
        grid=(b,),
        in_specs=[
            pl.BlockSpec((t, W_HEADS), lambda i: (i, G_IQ)),
            pl.BlockSpec((t, LANES), lambda i: (i, 0)),
            pl.BlockSpec((None, None, D_IDX, past), lambda i: (layer, i, 0, 0)),
            pl.BlockSpec((None, None, W_HEADS, past), lambda i: (layer, i, 0, 0)),
            pl.BlockSpec((None, None, W_HEADS, past), lambda i: (layer, i, 0, 0)),
            pl.BlockSpec((t, LANES), lambda i: (i, G_MISC * npair)),
            pl.BlockSpec((t, W_HEADS), lambda i: (i, G_CQ)),
            pl.BlockSpec((t, W_HEADS), lambda i: (i, G_CK)),
            pl.BlockSpec((t, W_HEADS), lambda i: (i, G_CV)),
        ],
        out_specs=pl.BlockSpec((t, W_HEADS), lambda i: (i, 0)),
        out_shape=jax.ShapeDtypeStruct((b * t, W_HEADS), BF16),
        scratch_shapes=[pltpu.VMEM((n_l, LANES), BF16), pltpu.VMEM((n_l, W_HEADS), BF16),
                        pltpu.VMEM((n_l, W_HEADS), BF16)] + _dsa_scratch(t, n_l),
        compiler_params=_cparams(1),
        name="dsa_sample",
    )(slab2, misc, ikd_past, cache_k, cache_v, slab2, slab2, slab2, slab2)


def _layernorm(z, g_ref, b_ref):
    mu = jnp.mean(z, axis=1, keepdims=True)
    d = z - mu
    var = jnp.mean(jnp.square(d), axis=1, keepdims=True)
    return d * lax.rsqrt(var + LN_EPS) * g_ref[...] + b_ref[...]


def _merge_kernel(x_ref, ya_ref, yb_ref, yc_ref, wgl_ref, bg_ref, wa_ref, wb_ref, wc_ref, wo_ref, g_ref, b_ref, o_ref):
    x = x_ref[...]
    xb = x.astype(BF16)
    merged = None
    for i, (y_ref, w_ref) in enumerate(((ya_ref, wa_ref), (yb_ref, wb_ref), (yc_ref, wc_ref))):
        gl = jnp.dot(xb, wgl_ref[:, i * D_MODEL:(i + 1) * D_MODEL], preferred_element_type=F32)
        gate = jax.nn.sigmoid(gl + bg_ref[i:i + 1, :])
        term = gate * jnp.dot(y_ref[...], w_ref[...], preferred_element_type=F32)
        merged = term if merged is None else merged + term
    z = DEEPNORM_ALPHA * x + jnp.dot(merged.astype(BF16), wo_ref[...], preferred_element_type=F32)
    o_ref[...] = _layernorm(z, g_ref, b_ref)


def _resident(shape, layer):
    nd = len(shape)
    return pl.BlockSpec((None,) + tuple(shape), lambda i: (layer,) + (0,) * nd, pipeline_mode=pl.Buffered(1))


def _merge(x, ya, yb, yc, wts, layer):
    m = x.shape[0]
    tm = min(TM_DENSE, m)
    row = lambda w: pl.BlockSpec((tm, w), lambda i: (i, 0))
    return pl.pallas_call(
        _merge_kernel,
        grid=(m // tm,),
        in_specs=[
            row(D_MODEL), row(W_HEADS), row(W_HEADS), row(W_HEADS),
            _resident((D_MODEL, N_BRANCH * D_MODEL), layer),
            _resident((N_BRANCH, D_MODEL), layer),
            _resident((W_HEADS, D_MODEL), layer),
            _resident((W_HEADS, D_MODEL), layer),
            _resident((W_HEADS, D_MODEL), layer),
            _resident((D_MODEL, D_MODEL), layer),
            _resident((1, D_MODEL), layer),
            _resident((1, D_MODEL), layer),
        ],
        out_specs=row(D_MODEL),
        out_shape=jax.ShapeDtypeStruct((m, D_MODEL), F32),
        compiler_params=_cparams(1),
        name="merge",
    )(x, ya, yb, yc, wts["w_gl"], wts["b_gate"], wts["w_br_a"], wts["w_br_b"], wts["w_br_c"], wts["w_o"],
      wts["ln1_g"], wts["ln1_b"])


def _ffn_kernel(x_ref, w1_ref, w3_ref, w2_ref, g_ref, b_ref, o_ref, *, d_ff, fc):
    x = x_ref[...]
    xb = x.astype(BF16)
    acc = None
    for c in range(d_ff // fc):
        sl = slice(c * fc, (c + 1) * fc)
        h1 = jnp.dot(xb, w1_ref[:, sl], preferred_element_type=F32)
        h3 = jnp.dot(xb, w3_ref[:, sl], preferred_element_type=F32)
        u = (h1 * jax.nn.sigmoid(h1) * h3).astype(BF16)
        term = jnp.dot(u, w2_ref[sl, :], preferred_element_type=F32)
        acc = term if acc is None else acc + term
    o_ref[...] = _layernorm(DEEPNORM_ALPHA * x + acc, g_ref, b_ref)


def _ffn(x, wts, layer):
    m = x.shape[0]
    tm = min(TM_DENSE, m)
    d_ff = wts["w_ff1"].shape[2]
    row = pl.BlockSpec((tm, D_MODEL), lambda i: (i, 0))
    return pl.pallas_call(
        functools.partial(_ffn_kernel, d_ff=d_ff, fc=256),
        grid=(m // tm,),
        in_specs=[
            row,
            _resident((D_MODEL, d_ff), layer),
            _resident((D_MODEL, d_ff), layer),
            _resident((d_ff, D_MODEL), layer),
            _resident((1, D_MODEL), layer),
            _resident((1, D_MODEL), layer),
        ],
        out_specs=row,
        out_shape=jax.ShapeDtypeStruct((m, D_MODEL), F32),
        compiler_params=_cparams(1),
        name="ffn",
    )(x, wts["w_ff1"], wts["w_ff3"], wts["w_ff2"], wts["ln2_g"], wts["ln2_b"])


def _rope_tables(pos, tm):
    half = HEAD_DIM // 2
    inv = ROPE_THETA ** (-jnp.arange(half, dtype=F32) / half)
    ang = pos.astype(F32)[:, None] * inv[None, :]
    cos, sin = jnp.cos(ang), jnp.sin(ang)
    cos_t = jnp.concatenate([cos, cos, cos, cos], axis=1)
    sin_t = jnp.concatenate([-sin, sin, -sin, sin], axis=1)
    reps = max(1, tm // pos.shape[0])
    return jnp.tile(cos_t, (reps, 1)), jnp.tile(sin_t, (reps, 1))


def _prep_weights(w_in, b_fgate, b_gate, diff_norm_g, w_br_a, w_br_b, w_br_c, w_o, ln1_g, ln1_b, ln2_g, ln2_b,
                  w_ff1, w_ff3, w_ff2, lam_q1, lam_k1, lam_q2, lam_k2):
    depth = w_in.shape[0]
    offs = np.cumsum((0,) + SPLIT_SIZES)
    col = lambda i: w_in[:, :, offs[i]:offs[i + 1]]
    aq, ak, av, af, bq, bk, bv, cq, ck, cv, iq, ik, iw, gl = (col(i) for i in range(len(SPLIT_SIZES)))
    qs = HEAD_DIM ** -0.5 * LOG2E
    misc_pad = jnp.zeros((depth, D_MODEL, W_HEADS - D_IDX - H_FOX - H_IDX), w_in.dtype)
    w_slab = jnp.concatenate([aq * qs, ak, av, bq * qs, bk, bv, cq * qs, ck, cv, iq * (D_IDX ** -0.5),
                              ik, af, iw, misc_pad], axis=2).astype(BF16)
    bf_row = jnp.zeros((depth, 1, LANES), F32).at[:, 0, MISC_LOGF:MISC_LOGF + H_FOX].set(b_fgate.astype(F32))
    return {
        "w_slab": w_slab,
        "bf_row": bf_row,
        "w_gl": gl.astype(BF16),
        "b_gate": b_gate.astype(F32),
        "w_br_a": w_br_a.astype(BF16), "w_br_b": w_br_b.astype(BF16), "w_br_c": w_br_c.astype(BF16),
        "w_o": w_o.astype(BF16),
        "ln1_g": ln1_g[:, None, :].astype(F32), "ln1_b": ln1_b[:, None, :].astype(F32),
        "ln2_g": ln2_g[:, None, :].astype(F32), "ln2_b": ln2_b[:, None, :].astype(F32),
        "w_ff1": w_ff1.astype(BF16), "w_ff3": w_ff3.astype(BF16), "w_ff2": w_ff2.astype(BF16),
        "lam4": jnp.stack([lam_q1, lam_k1, lam_q2, lam_k2], axis=1).astype(F32),
        "diff_g": diff_norm_g[:, None, :].astype(F32),
    }


def _layer_group(x, wts, layer, tabs, bsz, t, caches):
    cos_t, sin_t = tabs
    proj = _inproj(x, wts["w_slab"], layer, cos_t, sin_t, wts["bf_row"], t, caches is None)
    slab, misc = proj[:2]
    logf = misc[:, MISC_LOGF:MISC_LOGF + H_FOX].reshape(bsz, t, H_FOX)
    lf_all = misc.reshape(bsz, t, LANES)
    past = 0
    if caches is not None:
        past = caches["a_logf"].shape[2]
        past_lf = jnp.pad(caches["a_logf"][layer].astype(F32),
                          ((0, 0), (0, 0), (MISC_LOGF, LANES - MISC_LOGF - H_FOX)))
        lf_all = jnp.concatenate([past_lf, lf_all], axis=1)
    n_keys = past + t
    cum = _cumsum_time(lf_all)[:, :, MISC_LOGF:MISC_LOGF + H_FOX]
    npair = H_FOX // 2
    cum_rows = cum.transpose(0, 2, 1).reshape(bsz, npair, 2, n_keys)
    cum_cols = cum.reshape(bsz, n_keys, npair, 2).transpose(0, 2, 1, 3)
    lam_init = 0.8 - 0.6 * math.exp(-0.3 * layer)
    diff_extra = (wts["lam4"][layer], wts["diff_g"][layer])
    k_sel = min(TOPK_MAX, n_keys // 4)
    if caches is None:
        slab3 = slab.reshape(bsz, t, SLAB_W)
        vt_slab = proj[2]
        ya = _pair_attn_prompt("fox", slab, slab3, vt_slab, G_AQ, G_AK, 0, (cum_rows, cum_cols), lam_init)
        yb = _pair_attn_prompt("diff", slab, slab3, vt_slab, G_BQ, G_BK, 1, diff_extra, lam_init)
        yc = _dsa_prompt(slab, slab3, vt_slab, misc, k_sel)
    else:
        cq4, ck4 = cum_cols[:, :, past:], cum_rows
        ya = _pair_attn_sample("fox", slab, t, caches["a_k"], caches["a_v"], layer, G_AQ, G_AK, G_AV,
                               (cq4, ck4), lam_init)
        yb = _pair_attn_sample("diff", slab, t, caches["b_k"], caches["b_v"], layer, G_BQ, G_BK, G_BV,
                               diff_extra, lam_init)
        yc = _dsa_sample(slab, misc, t, caches["c_idx"], caches["c_k"], caches["c_v"], layer, k_sel)
    x = _merge(x, ya, yb, yc, wts, layer)
    x = _ffn(x, wts, layer)
    if caches is None:
        akt, avt, bkt, bv, ckt, cvt, misct = proj[3:]
        new_rows = (akt, avt, misct[:, MISC_LOGF:MISC_LOGF + H_FOX], bkt, bv, ckt, cvt, misct[:, :D_IDX])
    else:
        ak, av, bk, bv, ck, cv = proj[2:]
        new_rows = (
            ak.reshape(bsz, t, H_FOX, HEAD_DIM), av.reshape(bsz, t, H_FOX, HEAD_DIM), logf,
            bk.reshape(bsz, t, H_DIFF, 2, HEAD_DIM), bv.reshape(bsz, t, H_DIFF, 2 * HEAD_DIM),
            ck.reshape(bsz, t, H_DSA, HEAD_DIM), cv.reshape(bsz, t, H_DSA, HEAD_DIM),
            misc[:, :D_IDX].reshape(bsz, t, D_IDX),
        )
    return x, new_rows


def _prompt_outputs(rows, bsz, t):
    depth = len(rows)
    st = lambda i: jnp.stack([r[i] for r in rows])
    heads = lambda a, *dims: jnp.moveaxis(a.reshape((depth, bsz) + dims + (t,)), -1, 2)
    return [
        heads(st(0), H_FOX, HEAD_DIM), heads(st(1), H_FOX, HEAD_DIM), heads(st(2), H_FOX),
        heads(st(3), H_DIFF, 2, HEAD_DIM), st(4).reshape(depth, bsz, t, H_DIFF, 2 * HEAD_DIM),
        heads(st(5), H_DSA, HEAD_DIM), heads(st(6), H_DSA, HEAD_DIM), heads(st(7), D_IDX),
    ]


def kernel(x_prompt, x_sample, cache_a_k, cache_a_v, cache_a_logf, cache_b_k, cache_b_v, cache_c_k, cache_c_v, cache_c_idx, w_in, b_fgate, b_gate, lam_q1, lam_k1, lam_q2, lam_k2, diff_norm_g, w_br_a, w_br_b, w_br_c, w_o, ln1_g, ln1_b, ln2_g, ln2_b, w_ff1, w_ff3, w_ff2):
    depth = w_in.shape[0]
    bp, tp, _ = x_prompt.shape
    bs, ts, _ = x_sample.shape
    past = cache_a_k.shape[2]
    wts = _prep_weights(w_in, b_fgate, b_gate, diff_norm_g, w_br_a, w_br_b, w_br_c, w_o, ln1_g, ln1_b, ln2_g, ln2_b,
                        w_ff1, w_ff3, w_ff2, lam_q1, lam_k1, lam_q2, lam_k2)
    tabs_p = _rope_tables(jnp.arange(tp, dtype=I32), min(TM_DENSE, bp * tp))
    tabs_s = _rope_tables(past + jnp.arange(ts, dtype=I32), min(TM_DENSE, bs * ts))
    flat = lambda c: c.reshape(c.shape[0], c.shape[1], c.shape[2], -1)
    fmaj = lambda c: jnp.moveaxis(flat(c), 2, 3)
    caches = {
        "a_k": fmaj(cache_a_k), "a_v": fmaj(cache_a_v), "a_logf": cache_a_logf,
        "b_k": fmaj(cache_b_k), "b_v": flat(cache_b_v),
        "c_k": fmaj(cache_c_k), "c_v": fmaj(cache_c_v), "c_idx": fmaj(cache_c_idx),
    }
    yp = x_prompt.reshape(bp * tp, D_MODEL)
    ys = x_sample.reshape(bs * ts, D_MODEL)
    rows_p, rows_s = [], []
    for layer in range(depth):
        yp, rp = _layer_group(yp, wts, layer, tabs_p, bp, tp, None)
        ys, rs = _layer_group(ys, wts, layer, tabs_s, bs, ts, caches)
        rows_p.append(rp)
        rows_s.append(rs)
    outs_p = _prompt_outputs(rows_p, bp, tp)
    outs_s = [jnp.stack([r[i] for r in rows_s]) for i in range(8)]
    return (yp.reshape(bp, tp, D_MODEL), ys.reshape(bs, ts, D_MODEL), *outs_p, *outs_s)
```

```python
import functools
import math

import jax
import jax.numpy as jnp
import numpy as np
from jax import lax
from jax.experimental import pallas as pl
from jax.experimental.pallas import tpu as pltpu

F32 = jnp.float32
BF16 = jnp.bfloat16
I32 = jnp.int32

D_MODEL = 1024
HEAD_DIM = 64
H_FOX = 8
H_DIFF = 4
H_DSA = 8
H_IDX = 8
D_IDX = 64
CHUNK = 64
CHUNK_SHIFT = 6
TOPK_MAX = 256
ROPE_THETA = 10000.0
N_BRANCH = 3
LN_EPS = 1e-5
MODEL_DEPTH = 4
DEEPNORM_ALPHA = (2 * MODEL_DEPTH) ** 0.25
LOG2E = math.log2(math.e)
W_HEADS = 512
SPLIT_SIZES = (W_HEADS, W_HEADS, W_HEADS, H_FOX, W_HEADS, W_HEADS, W_HEADS, W_HEADS, W_HEADS, W_HEADS,
               H_IDX * D_IDX, D_IDX, H_IDX, N_BRANCH * D_MODEL)

LANES = 128
N_GROUPS = 11
SLAB_W = N_GROUPS * W_HEADS
G_AQ, G_AK, G_AV, G_BQ, G_BK, G_BV, G_CQ, G_CK, G_CV, G_IQ, G_MISC = range(N_GROUPS)
MISC_LOGF = 64
MISC_IW = 72
VMEM_LIMIT = 56 * 1024 * 1024
NEG_INF = float("-inf")
INT_MIN = np.int32(-2 ** 31)
INT_MAX_MASK = np.int32(2 ** 31 - 1)

TM_DENSE = 512
TQ_ATTN = 512
TQ_DSA = 512
SEG_DSA = 512
CH_DSA = 512


def _cparams(n_axes):
    return pltpu.CompilerParams(dimension_semantics=("arbitrary",) * n_axes, vmem_limit_bytes=VMEM_LIMIT)


def _lane_iota(n=LANES):
    return lax.broadcasted_iota(I32, (1, n), 1)


def _inproj_kernel(x_ref, w_ref, wn_ref, cos_ref, sin_ref, bf_ref, s16_ref, misc_ref, *rest, feature_major):
    if feature_major:
        vt_ref, akt_ref, avt_ref, bkt_ref, bv_ref, ckt_ref, cvt_ref, misct_ref, xb_ref, h_ref = rest
        t_out = {G_AK: akt_ref, G_AV: avt_ref, G_BK: bkt_ref, G_CK: ckt_ref, G_CV: cvt_ref}
        r_out = {G_BV: bv_ref}
    else:
        ak_ref, av_ref, bk_ref, bv_ref, ck_ref, cv_ref, xb_ref, h_ref = rest
        t_out = {}
        r_out = {G_AK: ak_ref, G_AV: av_ref, G_BK: bk_ref, G_BV: bv_ref, G_CK: ck_ref, G_CV: cv_ref}
    j = pl.program_id(1)
    slot = j & 1

    def project(w):
        return jnp.dot(xb_ref[...], w[...], preferred_element_type=F32)

    @pl.when(j == 0)
    def _():
        xb_ref[...] = x_ref[...].astype(BF16)
        h_ref[0] = project(w_ref)

    lane = _lane_iota()
    sel_up = (lane & (HEAD_DIM - 1)) < HEAD_DIM // 2

    def rope128(xb):
        up = pltpu.roll(xb, LANES - HEAD_DIM // 2, 1)
        dn = pltpu.roll(xb, HEAD_DIM // 2, 1)
        return xb * cos_ref[...] + jnp.where(sel_up, up, dn) * sin_ref[...]

    def finish(g, h):
        if g == G_MISC:
            hb = h[:, :LANES]
            roped = rope128(hb)
            z = hb + bf_ref[...]
            logf = jnp.minimum(z, 0.0) - jnp.log1p(jnp.exp(-jnp.abs(z)))
            misc = jnp.where(lane < MISC_LOGF, roped, jnp.where(lane < MISC_IW, logf, hb))
            misc_ref[...] = misc
            if feature_major:
                misct_ref[0] = misc.T
            ikd = jnp.where(lane < D_IDX, roped, pltpu.roll(roped, D_IDX, 1))
            s16_ref[:, :LANES] = ikd.astype(BF16)
            s16_ref[:, LANES:] = jnp.zeros((s16_ref.shape[0], W_HEADS - LANES), BF16)
            return
        if g in (G_BQ, G_BK, G_CQ, G_CK, G_IQ):
            val = jnp.concatenate([rope128(h[:, c * LANES:(c + 1) * LANES]) for c in range(W_HEADS // LANES)], axis=1)
        else:
            val = h
        s16_ref[...] = val.astype(BF16)
        if g in r_out:
            r_out[g][...] = val
        if feature_major and (g in t_out or g == G_BV):
            val_t = val.T
            if g in t_out:
                t_out[g][0] = val_t
            if g in (G_AV, G_BV, G_CV):
                vt_ref[0] = val_t.astype(BF16)

    for g in range(N_GROUPS):
        @pl.when(j == g)
        def _(g=g):
            h = h_ref[g & 1]
            if g + 1 < N_GROUPS:
                h_ref[(g + 1) & 1] = project(wn_ref)
            finish(g, h)


def _inproj(x, w_all, layer, cos_t, sin_t, bf_row, seq, feature_major):
    m = x.shape[0]
    tm = min(TM_DENSE, m)
    n_tab = cos_t.shape[0] // tm
    tab_map = lambda i, j: (i % n_tab, 0)
    rows32 = (pl.BlockSpec((tm, W_HEADS), lambda i, j: (i, 0)), jax.ShapeDtypeStruct((m, W_HEADS), F32))
    out_specs = [pl.BlockSpec((tm, W_HEADS), lambda i, j: (i, j)), pl.BlockSpec((tm, LANES), lambda i, j: (i, 0))]
    out_shape = [jax.ShapeDtypeStruct((m, SLAB_W), BF16), jax.ShapeDtypeStruct((m, LANES), F32)]
    if feature_major:
        nt = seq // tm
        bsz = m // seq
        cols32 = (pl.BlockSpec((1, W_HEADS, tm), lambda i, j: (i // nt, 0, i % nt)),
                  jax.ShapeDtypeStruct((bsz, W_HEADS, seq), F32))
        outs = [(pl.BlockSpec((1, W_HEADS, tm), lambda i, j: (i // nt, jnp.minimum(j // 3, 2), i % nt)),
                 jax.ShapeDtypeStruct((bsz, 3 * W_HEADS, seq), BF16)),
                cols32, cols32, cols32, rows32, cols32, cols32,
                (pl.BlockSpec((1, LANES, tm), lambda i, j: (i // nt, 0, i % nt)),
                 jax.ShapeDtypeStruct((bsz, LANES, seq), F32))]
    else:
        outs = [rows32] * 6
    out_specs += [o[0] for o in outs]
    out_shape += [o[1] for o in outs]
    return pl.pallas_call(
        functools.partial(_inproj_kernel, feature_major=feature_major),
        grid=(m // tm, N_GROUPS),
        in_specs=[
            pl.BlockSpec((tm, D_MODEL), lambda i, j: (i, 0)),
            pl.BlockSpec((None, D_MODEL, W_HEADS), lambda i, j: (layer, 0, 0)),
            pl.BlockSpec((None, D_MODEL, W_HEADS), lambda i, j: (layer, 0, jnp.minimum(j + 1, N_GROUPS - 1))),
            pl.BlockSpec((tm, LANES), tab_map),
            pl.BlockSpec((tm, LANES), tab_map),
            pl.BlockSpec((None, 1, LANES), lambda i, j: (layer, 0, 0)),
        ],
        out_specs=out_specs,
        out_shape=out_shape,
        scratch_shapes=[pltpu.VMEM((tm, D_MODEL), BF16), pltpu.VMEM((2, tm, W_HEADS), F32)],
        compiler_params=_cparams(2),
        name="inproj",
    )(x, w_all, w_all, cos_t, sin_t, bf_row)


def _split3(x):
    hi = x.astype(BF16)
    r1 = x - hi.astype(F32)
    mid = r1.astype(BF16)
    lo = (r1 - mid.astype(F32)).astype(BF16)
    return hi, mid, lo


def _cumsum_kernel(x_ref, o_ref, carry_ref):
    t = pl.program_id(1)

    @pl.when(t == 0)
    def _():
        carry_ref[...] = jnp.zeros_like(carry_ref)

    x = x_ref[0]
    tl = x.shape[0]
    r = lax.broadcasted_iota(I32, (tl, tl), 0)
    c = lax.broadcasted_iota(I32, (tl, tl), 1)
    tri = jnp.where(c <= r, 1.0, 0.0).astype(BF16)
    hi, mid, lo = _split3(x)
    dot = lambda a: jnp.dot(tri, a, preferred_element_type=F32)
    cs = (dot(lo) + dot(mid)) + dot(hi) + carry_ref[...]
    o_ref[0] = cs * LOG2E
    carry_ref[...] = cs[tl - 1:tl, :]


def _cumsum_time(x):
    b, l, h = x.shape
    tl = 512 if l % 512 == 0 else l
    return pl.pallas_call(
        _cumsum_kernel,
        grid=(b, l // tl),
        in_specs=[pl.BlockSpec((1, tl, h), lambda i, t: (i, t, 0))],
        out_specs=pl.BlockSpec((1, tl, h), lambda i, t: (i, t, 0)),
        out_shape=jax.ShapeDtypeStruct((b, l, h), F32),
        scratch_shapes=[pltpu.VMEM((1, h), F32)],
        compiler_params=_cparams(2),
        name="cumsum",
    )(x)


def _split_pair(q):
    lane = _lane_iota()
    zero = jnp.zeros_like(q)
    return jnp.where(lane < HEAD_DIM, q, zero), jnp.where(lane >= HEAD_DIM, q, zero)


def _qk(q, k):
    return lax.dot_general(q, k, (((1,), (1,)), ((), ())), preferred_element_type=F32)


def _online_update(s, v, m_old, l_old, acc_old):
    m_new = jnp.maximum(m_old, jnp.max(s, axis=1, keepdims=True))
    m_safe = jnp.where(m_new == NEG_INF, 0.0, m_new)
    alpha = jnp.exp2(m_old - m_safe)
    p = jnp.exp2(s - m_safe)
    l_new = alpha * l_old + jnp.sum(p, axis=1, keepdims=True)
    acc_new = alpha * acc_old + jnp.dot(p.astype(BF16), v, preferred_element_type=F32)
    return m_new, l_new, acc_new


def _causal_mask(kind, rows, cols, row0, col0):
    r = row0 + lax.broadcasted_iota(I32, (rows, 1), 0)
    c = col0 + lax.broadcasted_iota(I32, (1, cols), 1)
    if kind == "fox":
        return c <= r
    return (c >> CHUNK_SHIFT) <= (r >> CHUNK_SHIFT)


def _diff_lambda(lam_ref, lam_init):
    lp = lam_ref[...]
    s1 = jnp.sum(lp[0:1] * lp[1:2], axis=1, keepdims=True)
    s2 = jnp.sum(lp[2:3] * lp[3:4], axis=1, keepdims=True)
    return jnp.exp(s1) - jnp.exp(s2) + lam_init


def _pair_epilogue(kind, o0, o1, lam_ref, g_ref, lam_init):
    if kind == "fox":
        return jnp.where(_lane_iota() < HEAD_DIM, o0, o1)
    o = o0 - _diff_lambda(lam_ref, lam_init) * o1
    o = o * lax.rsqrt(jnp.mean(jnp.square(o), axis=1, keepdims=True) + LN_EPS) * g_ref[...]
    return o * (1.0 - lam_init)


SUM_ROWS = 16


def _with_sum_rows(vt):
    return jnp.concatenate([vt, jnp.ones((SUM_ROWS, vt.shape[1]), vt.dtype)], axis=0)


def _online_update_t(st, vt_aug, m_old, acc_old):
    m_new = jnp.maximum(m_old, jnp.max(st, axis=0, keepdims=True))
    m_safe = jnp.where(m_new == NEG_INF, 0.0, m_new)
    alpha = jnp.exp2(m_old - m_safe)
    pt = jnp.exp2(st - m_safe).astype(BF16)
    acc_new = alpha * acc_old + jnp.dot(vt_aug, pt, preferred_element_type=F32)
    return m_new, acc_new


def _normalised(acc):
    return acc[:LANES] * (1.0 / acc[LANES:LANES + 1])


def _causal_mask_t(kind, keys, queries):
    kp = lax.broadcasted_iota(I32, (keys, 1), 0)
    qp = lax.broadcasted_iota(I32, (1, queries), 1)
    if kind == "fox":
        return kp <= qp
    return (kp >> CHUNK_SHIFT) <= (qp >> CHUNK_SHIFT)


def _init_stats(m_ref, acc_ref):
    m_ref[...] = jnp.full(m_ref.shape, NEG_INF, F32)
    acc_ref[...] = jnp.zeros(acc_ref.shape, F32)


def _pair_attn_prompt_kernel(*refs, kind, tq, lam_init):
    if kind == "fox":
        q_ref, k_ref, vt_ref, cq_ref, ck_ref, o_ref, m_ref, acc_ref, st_ref = refs
        lam_ref = g_ref = None
    else:
        q_ref, k_ref, vt_ref, lam_ref, g_ref, o_ref, m_ref, acc_ref, st_ref = refs
    qi = pl.program_id(2)
    qs = _split_pair(q_ref[...])
    _init_stats(m_ref, acc_ref)

    def scores(j, slot):
        start = pl.multiple_of(j * tq, tq)
        k = k_ref[0, pl.ds(start, tq), :]
        for h in range(2):
            st = _qk(k, qs[h])
            if kind == "fox":
                st = st + cq_ref[0, 0, h:h + 1, :] - ck_ref[0, 0, pl.ds(start, tq), h:h + 1]
            st_ref[slot, h] = st

    def consume(j, slot, masked):
        start = pl.multiple_of(j * tq, tq)
        vt = _with_sum_rows(vt_ref[0, :, pl.ds(start, tq)])
        for h in range(2):
            st = st_ref[slot, h]
            if masked:
                st = jnp.where(_causal_mask_t(kind, tq, tq), st, NEG_INF)
            m_ref[h], acc_ref[h] = _online_update_t(st, vt, m_ref[h], acc_ref[h])

    scores(0, 0)

    def pair(jj, carry):
        j = 2 * jj
        consume(j, 0, False)
        scores(j + 1, 1)
        consume(j + 1, 1, False)
        scores(j + 2, 0)
        return carry

    lax.fori_loop(0, qi // 2, pair, 0)

    @pl.when(qi % 2 == 1)
    def _():
        consume(qi - 1, 0, False)
        scores(qi, 1)
        consume(qi, 1, True)

    @pl.when(qi % 2 == 0)
    def _():
        consume(qi, 0, True)

    o0 = _normalised(acc_ref[0]).T
    o1 = _normalised(acc_ref[1]).T
    o_ref[...] = _pair_epilogue(kind, o0, o1, lam_ref, g_ref, lam_init).astype(BF16)


def _pair_attn_prompt(kind, slab2, slab3, vt_slab, qg, kg, vt_group, extra, lam_init):
    b, t, _ = slab3.shape
    tq = min(TQ_ATTN, t)
    nq = t // tq
    npair = W_HEADS // LANES
    in_specs = [
        pl.BlockSpec((tq, LANES), lambda i, p, q: (i * nq + q, qg * npair + p)),
        pl.BlockSpec((1, t, LANES), lambda i, p, q: (i, 0, kg * npair + p)),
        pl.BlockSpec((1, LANES, t), lambda i, p, q: (i, vt_group * npair + p, 0)),
    ]
    if kind == "fox":
        in_specs += [
            pl.BlockSpec((1, 1, 2, tq), lambda i, p, q: (i, p, 0, q)),
            pl.BlockSpec((1, 1, t, 2), lambda i, p, q: (i, p, 0, 0)),
        ]
    else:
        in_specs += [
            pl.BlockSpec((4, HEAD_DIM), lambda i, p, q: (0, 0)),
            pl.BlockSpec((1, LANES), lambda i, p, q: (0, 0)),
        ]
    return pl.pallas_call(
        functools.partial(_pair_attn_prompt_kernel, kind=kind, tq=tq, lam_init=lam_init),
        grid=(b, npair, nq),
        in_specs=in_specs,
        out_specs=pl.BlockSpec((tq, LANES), lambda i, p, q: (i * nq + q, p)),
        out_shape=jax.ShapeDtypeStruct((b * t, W_HEADS), BF16),
        scratch_shapes=[pltpu.VMEM((2, 1, tq), F32), pltpu.VMEM((2, LANES + SUM_ROWS, tq), F32),
                        pltpu.VMEM((2, 2, tq, tq), F32)],
        compiler_params=_cparams(3),
        name=kind + "_prompt",
    )(slab2, slab3, vt_slab, *extra)


def _pair_attn_sample_kernel(*refs, kind, t, past, lam_init):
    if kind == "fox":
        q_ref, kp_ref, vp_ref, kn_ref, vn_ref, cq_ref, ck_ref, o_ref = refs
        lam_ref = g_ref = None
    else:
        q_ref, kp_ref, vp_ref, kn_ref, vn_ref, lam_ref, g_ref, o_ref = refs
    qs = _split_pair(q_ref[...])
    kp = kp_ref[...].T.astype(BF16)
    vp = (vp_ref[...].T if kind == "fox" else vp_ref[...]).astype(BF16)
    kn = kn_ref[...]
    vn = vn_ref[...]
    outs = []
    for h in range(2):
        m = jnp.full((t, 1), NEG_INF, F32)
        l = jnp.zeros((t, 1), F32)
        acc = jnp.zeros((t, LANES), F32)
        s = _qk(qs[h], kp)
        if kind == "fox":
            s = s + cq_ref[0, 0, :, h:h + 1] - ck_ref[0, 0, h:h + 1, :past]
        m, l, acc = _online_update(s, vp, m, l, acc)
        s = _qk(qs[h], kn)
        if kind == "fox":
            s = s + cq_ref[0, 0, :, h:h + 1] - ck_ref[0, 0, h:h + 1, past:past + t]
        s = jnp.where(_causal_mask(kind, t, t, past, past), s, NEG_INF)
        m, l, acc = _online_update(s, vn, m, l, acc)
        outs.append(acc * (1.0 / l))
    o_ref[...] = _pair_epilogue(kind, outs[0], outs[1], lam_ref, g_ref, lam_init).astype(BF16)


def _pair_attn_sample(kind, slab2, t, cache_k, cache_v, layer, qg, kg, vg, extra, lam_init):
    b = slab2.shape[0] // t
    past = cache_k.shape[3]
    npair = W_HEADS // LANES
    feature_major = pl.BlockSpec((None, None, LANES, past), lambda i, p: (layer, i, p, 0))
    time_major = pl.BlockSpec((None, None, past, LANES), lambda i, p: (layer, i, 0, p))
    in_specs = [
        pl.BlockSpec((t, LANES), lambda i, p: (i, qg * npair + p)),
        feature_major,
        feature_major if kind == "fox" else time_major,
        pl.BlockSpec((t, LANES), lambda i, p: (i, kg * npair + p)),
        pl.BlockSpec((t, LANES), lambda i, p: (i, vg * npair + p)),
    ]
    if kind == "fox":
        in_specs += [
            pl.BlockSpec((1, 1, t, 2), lambda i, p: (i, p, 0, 0)),
            pl.BlockSpec((1, 1, 2, past + t), lambda i, p: (i, p, 0, 0)),
        ]
    else:
        in_specs += [
            pl.BlockSpec((4, HEAD_DIM), lambda i, p: (0, 0)),
            pl.BlockSpec((1, LANES), lambda i, p: (0, 0)),
        ]
    return pl.pallas_call(
        functools.partial(_pair_attn_sample_kernel, kind=kind, t=t, past=past, lam_init=lam_init),
        grid=(b, npair),
        in_specs=in_specs,
        out_specs=pl.BlockSpec((t, LANES), lambda i, p: (i, p)),
        out_shape=jax.ShapeDtypeStruct((b * t, W_HEADS), BF16),
        compiler_params=_cparams(2),
        name=kind + "_sample",
    )(slab2, cache_k, cache_v, slab2, slab2, *extra)


def _dsa_core(iq_ref, iw, ikd_ref, cq_ref, ck_ref, cv_ref, o_ref, keys_ref, bias_ref, m_ref, l_ref, acc_ref,
              *, tq, n_l, ch, qpos0, n_keys, k_sel):
    nch = n_l // ch
    lane = _lane_iota()
    qchunk = (qpos0 + lax.broadcasted_iota(I32, (tq, 1), 0)) >> CHUNK_SHIFT
    w = [iw[:, MISC_IW + h:MISC_IW + h + 1] for h in range(H_IDX)]

    def score_chunk(c, carry):
        start = pl.multiple_of(c * ch, ch)
        ik = ikd_ref[pl.ds(start, ch), :]
        acc = jnp.zeros((tq, ch), F32)
        for hp in range(H_IDX // 2):
            pair = _split_pair(iq_ref[:, hp * LANES:(hp + 1) * LANES])
            for half in range(2):
                acc = acc + jnp.maximum(_qk(pair[half], ik), 0.0) * w[2 * hp + half]
        score = acc * (H_IDX ** -0.5) + 0.0
        bits = lax.bitcast_convert_type(score, I32)
        key = bits ^ ((bits >> 31) & INT_MAX_MASK)
        kpos = start + lax.broadcasted_iota(I32, (1, ch), 1)
        adm = ((kpos >> CHUNK_SHIFT) <= qchunk) & (kpos < n_keys)
        keys_ref[:, pl.ds(start, ch)] = jnp.where(adm, key, INT_MIN)
        return carry

    lax.fori_loop(0, nch, score_chunk, 0)

    def count(pred):
        def body(c, a):
            start = pl.multiple_of(c * ch, ch)
            hit = jnp.where(pred(keys_ref[:, pl.ds(start, ch)]), 1.0, 0.0)
            for t in range(ch // LANES):
                a = a + hit[:, t * LANES:(t + 1) * LANES]
            return a
        part = lax.fori_loop(0, nch, body, jnp.zeros((tq, LANES), F32))
        return jnp.sum(part, axis=1, keepdims=True)

    def bit_step(i, thr_u):
        cand_u = thr_u | lax.shift_left(jnp.int32(1), 31 - i)
        cand_s = cand_u ^ INT_MIN
        n_ge = count(lambda kk: kk >= cand_s)
        return jnp.where(n_ge >= k_sel, cand_u, thr_u)

    thr = lax.fori_loop(0, 32, bit_step, jnp.zeros((tq, 1), I32)) ^ INT_MIN
    need = k_sel - count(lambda kk: kk > thr)

    r128 = lax.broadcasted_iota(I32, (LANES, LANES), 0)
    c128 = lax.broadcasted_iota(I32, (LANES, LANES), 1)
    before = jnp.where(r128 < c128, 1.0, 0.0).astype(BF16)

    def select_block(t, seen):
        start = pl.multiple_of(t * LANES, LANES)
        kk = keys_ref[:, pl.ds(start, LANES)]
        eq = kk == thr
        eqf = jnp.where(eq, 1.0, 0.0)
        rank = seen + jnp.dot(eqf.astype(BF16), before, preferred_element_type=F32)
        sel = ((kk > thr) | (eq & (rank < need))) & (kk != INT_MIN)
        bias_ref[:, pl.ds(start, LANES)] = jnp.where(sel, 0.0, NEG_INF)
        return seen + jnp.sum(eqf, axis=1, keepdims=True)

    lax.fori_loop(0, n_l // LANES, select_block, jnp.zeros((tq, 1), F32))

    for p in range(H_DSA // 2):
        sl = slice(p * LANES, (p + 1) * LANES)
        qs = _split_pair(cq_ref[:, sl])
        m_ref[...] = jnp.full(m_ref.shape, NEG_INF, F32)
        l_ref[...] = jnp.zeros(l_ref.shape, F32)
        acc_ref[...] = jnp.zeros(acc_ref.shape, F32)

        def attend(c, carry):
            start = pl.multiple_of(c * ch, ch)
            k = ck_ref[pl.ds(start, ch), sl]
            v = cv_ref[pl.ds(start, ch), sl]
            bias = bias_ref[:, pl.ds(start, ch)]
            for h in range(2):
                s = _qk(qs[h], k) + bias
                m_ref[h], l_ref[h], acc_ref[h] = _online_update(s, v, m_ref[h], l_ref[h], acc_ref[h])
            return carry

        lax.fori_loop(0, nch, attend, 0)
        o0 = acc_ref[0] * (1.0 / l_ref[0])
        o1 = acc_ref[1] * (1.0 / l_ref[1])
        o_ref[:, sl] = jnp.where(lane < HEAD_DIM, o0, o1).astype(BF16)


def _dsa_scratch(tq, n_l):
    return [pltpu.VMEM((tq, n_l), I32), pltpu.VMEM((tq, n_l), F32),
            pltpu.VMEM((2, tq, 1), F32), pltpu.VMEM((2, tq, 1), F32), pltpu.VMEM((2, tq, LANES), F32)]


def _dsa_core_t(iq_ref, iw_t, ikd_ref, cq_ref, ck_ref, cvt_ref, o_ref, keys_ref, bias_ref, m_ref, acc_ref, st_ref,
                *, tq, n_l, ch, qpos0, n_keys, k_sel):
    nch = n_l // ch
    qchunk = (qpos0 + _lane_iota(tq)) >> CHUNK_SHIFT
    w = [iw_t[MISC_IW + h:MISC_IW + h + 1, :] for h in range(H_IDX)]
    iqs = []
    for hp in range(H_IDX // 2):
        iqs.extend(_split_pair(iq_ref[:, hp * LANES:(hp + 1) * LANES]))

    def score_chunk(c, carry):
        start = pl.multiple_of(c * ch, ch)
        ik = ikd_ref[pl.ds(start, ch), :]
        acc = jnp.zeros((ch, tq), F32)
        for h in range(H_IDX):
            acc = acc + jnp.maximum(_qk(ik, iqs[h]), 0.0) * w[h]
        score = acc * (H_IDX ** -0.5) + 0.0
        bits = lax.bitcast_convert_type(score, I32)
        key = bits ^ ((bits >> 31) & INT_MAX_MASK)
        kpos = start + lax.broadcasted_iota(I32, (ch, 1), 0)
        adm = ((kpos >> CHUNK_SHIFT) <= qchunk) & (kpos < n_keys)
        keys_ref[pl.ds(start, ch), :] = jnp.where(adm, key, INT_MIN)
        return carry

    lax.fori_loop(0, nch, score_chunk, 0)

    n_acc = 4

    def count(pred):
        def body(c, accs):
            start = pl.multiple_of(c * ch, ch)
            kk = keys_ref[pl.ds(start, ch), :]
            accs = list(accs)
            for g in range(ch // 8):
                a = accs[g % n_acc]
                accs[g % n_acc] = jnp.where(pred(kk[g * 8:(g + 1) * 8, :]), a + 1.0, a)
            return tuple(accs)
        accs = lax.fori_loop(0, nch, body, tuple(jnp.zeros((8, tq), F32) for _ in range(n_acc)))
        return jnp.sum((accs[0] + accs[1]) + (accs[2] + accs[3]), axis=0, keepdims=True)

    few = count(lambda kk: kk != INT_MIN) <= k_sel

    def search_on(state):
        i, _, n_thr = state
        open_rows = jnp.where((n_thr == k_sel) | few, 0, 1)
        return (i < 32) & (jnp.max(open_rows) > 0)

    def bit_step(state):
        i, thr_u, n_thr = state
        cand_u = thr_u | lax.shift_left(jnp.int32(1), 31 - i)
        cand_s = cand_u ^ INT_MIN
        n_ge = count(lambda kk: kk >= cand_s)
        take = n_ge >= k_sel
        return i + 1, jnp.where(take, cand_u, thr_u), jnp.where(take, n_ge, n_thr)

    _, thr_u, _ = lax.while_loop(search_on, bit_step,
                                 (jnp.int32(0), jnp.zeros((1, tq), I32), jnp.full((1, tq), float(n_l), F32)))
    thr = thr_u ^ INT_MIN
    need = k_sel - count(lambda kk: kk > thr)

    sb = 2 * LANES
    r_i = lax.broadcasted_iota(I32, (sb, sb), 0)
    c_i = lax.broadcasted_iota(I32, (sb, sb), 1)
    before = jnp.where(c_i < r_i, 1.0, 0.0).astype(BF16)

    def select_block(t, seen):
        start = pl.multiple_of(t * sb, sb)
        kk = keys_ref[pl.ds(start, sb), :]
        eq = kk == thr
        eqf = jnp.where(eq, 1.0, 0.0)
        rank = seen + jnp.dot(before, eqf.astype(BF16), preferred_element_type=F32)
        sel = ((kk > thr) | (eq & (rank < need))) & (kk != INT_MIN)
        bias_ref[pl.ds(start, sb), :] = jnp.where(sel, 0.0, NEG_INF)
        return seen + jnp.sum(eqf, axis=0, keepdims=True)

    lax.fori_loop(0, n_l // sb, select_block, jnp.zeros((1, tq), F32))

    head0_rows = lax.broadcasted_iota(I32, (LANES, 1), 0) < HEAD_DIM
    for p in range(H_DSA // 2):
        sl = slice(p * LANES, (p + 1) * LANES)
        qs = _split_pair(cq_ref[:, sl])
        _init_stats(m_ref, acc_ref)

        def scores(c, slot):
            start = pl.multiple_of(c * ch, ch)
            k = ck_ref[pl.ds(start, ch), sl]
            bias = bias_ref[pl.ds(start, ch), :]
            for h in range(2):
                st_ref[slot, h] = _qk(k, qs[h]) + bias

        def consume(c, slot):
            start = pl.multiple_of(c * ch, ch)
            vt = _with_sum_rows(cvt_ref[sl, pl.ds(start, ch)])
            for h in range(2):
                m_ref[h], acc_ref[h] = _online_update_t(st_ref[slot, h], vt, m_ref[h], acc_ref[h])

        scores(0, 0)

        def attend_pair(cc, carry):
            c = 2 * cc
            consume(c, 0)
            scores(c + 1, 1)
            consume(c + 1, 1)
            scores(c + 2, 0)
            return carry

        lax.fori_loop(0, (nch - 1) // 2, attend_pair, 0)
        if nch % 2 == 0:
            consume(nch - 2, 0)
            scores(nch - 1, 1)
            consume(nch - 1, 1)
        else:
            consume(nch - 1, 0)
        ot = jnp.where(head0_rows, _normalised(acc_ref[0]), _normalised(acc_ref[1]))
        o_ref[:, sl] = ot.T.astype(BF16)


def _dsa_prompt_kernel(iq_ref, misc_ref, ikd_ref, cq_ref, ck_ref, cvt_ref, o_ref, *scratch, tq, n_l, q0, n_keys, k_sel):
    qpos0 = q0 + pl.program_id(1) * tq
    _dsa_core_t(iq_ref, misc_ref[...].T, ikd_ref.at[0], cq_ref, ck_ref.at[0], cvt_ref.at[0], o_ref, *scratch,
                tq=tq, n_l=n_l, ch=min(CH_DSA, n_l), qpos0=qpos0, n_keys=n_keys, k_sel=k_sel)


def _dsa_prompt_segment(slab2, slab3, vt_slab, misc, seg, seg_len, k_sel):
    b, t, _ = slab3.shape
    tq = min(TQ_DSA, seg_len)
    nqs = seg_len // tq
    nq = t // tq
    n_l = (seg + 1) * seg_len
    row = lambda i, q: i * nq + seg * nqs + q
    npair = W_HEADS // LANES
    return pl.pallas_call(
        functools.partial(_dsa_prompt_kernel, tq=tq, n_l=n_l, q0=seg * seg_len, n_keys=t, k_sel=k_sel),
        grid=(b, nqs),
        in_specs=[
            pl.BlockSpec((tq, W_HEADS), lambda i, q: (row(i, q), G_IQ)),
            pl.BlockSpec((tq, LANES), lambda i, q: (row(i, q), 0)),
            pl.BlockSpec((1, n_l, LANES), lambda i, q: (i, 0, G_MISC * npair)),
            pl.BlockSpec((tq, W_HEADS), lambda i, q: (row(i, q), G_CQ)),
            pl.BlockSpec((1, n_l, W_HEADS), lambda i, q: (i, 0, G_CK)),
            pl.BlockSpec((1, W_HEADS, n_l), lambda i, q: (i, 2, 0)),
        ],
        out_specs=pl.BlockSpec((tq, W_HEADS), lambda i, q: (i * nqs + q, 0)),
        out_shape=jax.ShapeDtypeStruct((b * seg_len, W_HEADS), BF16),
        scratch_shapes=[pltpu.VMEM((n_l, tq), I32), pltpu.VMEM((n_l, tq), F32),
                        pltpu.VMEM((2, 1, tq), F32), pltpu.VMEM((2, LANES + SUM_ROWS, tq), F32),
                        pltpu.VMEM((2, 2, min(CH_DSA, n_l), tq), F32)],
        compiler_params=_cparams(2),
        name="dsa_prompt",
    )(slab2, misc, slab3, slab2, slab3, vt_slab)


def _dsa_prompt(slab2, slab3, vt_slab, misc, k_sel):
    b, t, _ = slab3.shape
    seg_len = min(SEG_DSA, t)
    segs = [_dsa_prompt_segment(slab2, slab3, vt_slab, misc, s, seg_len, k_sel) for s in range(t // seg_len)]
    y = jnp.stack([s.reshape(b, seg_len, W_HEADS) for s in segs], axis=1)
    return y.reshape(b * t, W_HEADS)


def _dsa_sample_kernel(iq_ref, misc_ref, ikp_ref, ckp_ref, cvp_ref, ikn_ref, cq_ref, ckn_ref, cvn_ref, o_ref,
                       ik_s, ck_s, cv_s, *scratch, t, past, n_l, ch, k_sel):
    pad = n_l - past - t
    ikp = ikp_ref[...]
    ik_s[:past, :] = jnp.concatenate([ikp, ikp], axis=0).T.astype(BF16)
    ik_s[past:past + t, :] = ikn_ref[...]
    ik_s[past + t:, :] = jnp.zeros((pad, LANES), BF16)
    ck_s[:past, :] = ckp_ref[...].T.astype(BF16)
    ck_s[past:past + t, :] = ckn_ref[...]
    ck_s[past + t:, :] = jnp.zeros((pad, W_HEADS), BF16)
    cv_s[:past, :] = cvp_ref[...].T.astype(BF16)
    cv_s[past:past + t, :] = cvn_ref[...]
    cv_s[past + t:, :] = jnp.zeros((pad, W_HEADS), BF16)
    _dsa_core(iq_ref, misc_ref[...], ik_s, cq_ref, ck_s, cv_s, o_ref, *scratch,
              tq=t, n_l=n_l, ch=ch, qpos0=past, n_keys=past + t, k_sel=k_sel)


def _dsa_sample(slab2, misc, t, ikd_past, cache_k, cache_v, layer, k_sel):
    b = slab2.shape[0] // t
    past = cache_k.shape[3]
    n_l = -(-(past + t) // LANES) * LANES
    ch = n_l
    npair = W_HEADS // LANES
    return pl.pallas_call(
        functools.partial(_dsa_sample_kernel, t=t, past=past, n_l=n_l, ch=ch, k_sel=k_sel),
        grid=(b,),
        in_specs=[
            pl.BlockSpec((t, W_HEADS), lambda i: (i, G_IQ)),
            pl.BlockSpec((t, LANES), lambda i: (i, 0)),
            pl.BlockSpec((None, None, D_IDX, past), lambda i: (layer, i, 0, 0)),
            pl.BlockSpec((None, None, W_HEADS, past), lambda i: (layer, i, 0, 0)),
            pl.BlockSpec((None, None, W_HEADS, past), lambda i: (layer, i, 0, 0)),
            pl.BlockSpec((t, LANES), lambda i: (i, G_MISC * npair)),
            pl.BlockSpec((t, W_HEADS), lambda i: (i, G_CQ)),
            pl.BlockSpec((t, W_HEADS), lambda i: (i, G_CK)),
            pl.BlockSpec((t, W_HEADS), lambda i: (i, G_CV)),
        ],
        out_specs=pl.BlockSpec((t, W_HEADS), lambda i: (i, 0)),
        out_shape=jax.ShapeDtypeStruct((b * t, W_HEADS), BF16),
        scratch_shapes=[pltpu.VMEM((n_l, LANES), BF16), pltpu.VMEM((n_l, W_HEADS), BF16),
                        pltpu.VMEM((n_l, W_HEADS), BF16)] + _dsa_scratch(t, n_l),
        compiler_params=_cparams(1),
        name="dsa_sample",
    )(slab2, misc, ikd_past, cache_k, cache_v, slab2, slab2, slab2, slab2)


def _layernorm(z, g_ref, b_ref):
    mu = jnp.mean(z, axis=1, keepdims=True)
    d = z - mu
    var = jnp.mean(jnp.square(d), axis=1, keepdims=True)
    return d * lax.rsqrt(var + LN_EPS) * g_ref[...] + b_ref[...]


def _merge_kernel(x_ref, ya_ref, yb_ref, yc_ref, wgl_ref, bg_ref, wa_ref, wb_ref, wc_ref, wo_ref, g_ref, b_ref, o_ref):
    x = x_ref[...]
    xb = x.astype(BF16)
    merged = None
    for i, (y_ref, w_ref) in enumerate(((ya_ref, wa_ref), (yb_ref, wb_ref), (yc_ref, wc_ref))):
        gl = jnp.dot(xb, wgl_ref[:, i * D_MODEL:(i + 1) * D_MODEL], preferred_element_type=F32)
        gate = jax.nn.sigmoid(gl + bg_ref[i:i + 1, :])
        term = gate * jnp.dot(y_ref[...], w_ref[...], preferred_element_type=F32)
        merged = term if merged is None else merged + term
    z = DEEPNORM_ALPHA * x + jnp.dot(merged.astype(BF16), wo_ref[...], preferred_element_type=F32)
    o_ref[...] = _layernorm(z, g_ref, b_ref)


def _resident(shape, layer):
    nd = len(shape)
    return pl.BlockSpec((None,) + tuple(shape), lambda i: (layer,) + (0,) * nd, pipeline_mode=pl.Buffered(1))


def _merge(x, ya, yb, yc, wts, layer):
    m = x.shape[0]
    tm = min(TM_DENSE, m)
    row = lambda w: pl.BlockSpec((tm, w), lambda i: (i, 0))
    return pl.pallas_call(
        _merge_kernel,
        grid=(m // tm,),
        in_specs=[
            row(D_MODEL), row(W_HEADS), row(W_HEADS), row(W_HEADS),
            _resident((D_MODEL, N_BRANCH * D_MODEL), layer),
            _resident((N_BRANCH, D_MODEL), layer),
            _resident((W_HEADS, D_MODEL), layer),
            _resident((W_HEADS, D_MODEL), layer),
            _resident((W_HEADS, D_MODEL), layer),
            _resident((D_MODEL, D_MODEL), layer),
            _resident((1, D_MODEL), layer),
            _resident((1, D_MODEL), layer),
        ],
        out_specs=row(D_MODEL),
        out_shape=jax.ShapeDtypeStruct((m, D_MODEL), F32),
        compiler_params=_cparams(1),
        name="merge",
    )(x, ya, yb, yc, wts["w_gl"], wts["b_gate"], wts["w_br_a"], wts["w_br_b"], wts["w_br_c"], wts["w_o"],
      wts["ln1_g"], wts["ln1_b"])


def _ffn_kernel(x_ref, w1_ref, w3_ref, w2_ref, g_ref, b_ref, o_ref, *, d_ff, fc):
    x = x_ref[...]
    xb = x.astype(BF16)
    acc = None
    for c in range(d_ff // fc):
        sl = slice(c * fc, (c + 1) * fc)
        h1 = jnp.dot(xb, w1_ref[:, sl], preferred_element_type=F32)
        h3 = jnp.dot(xb, w3_ref[:, sl], preferred_element_type=F32)
        u = (h1 * jax.nn.sigmoid(h1) * h3).astype(BF16)
        term = jnp.dot(u, w2_ref[sl, :], preferred_element_type=F32)
        acc = term if acc is None else acc + term
    o_ref[...] = _layernorm(DEEPNORM_ALPHA * x + acc, g_ref, b_ref)


def _ffn(x, wts, layer):
    m = x.shape[0]
    tm = min(TM_DENSE, m)
    d_ff = wts["w_ff1"].shape[2]
    row = pl.BlockSpec((tm, D_MODEL), lambda i: (i, 0))
    return pl.pallas_call(
        functools.partial(_ffn_kernel, d_ff=d_ff, fc=256),
        grid=(m // tm,),
        in_specs=[
            row,
            _resident((D_MODEL, d_ff), layer),
            _resident((D_MODEL, d_ff), layer),
            _resident((d_ff, D_MODEL), layer),
            _resident((1, D_MODEL), layer),
            _resident((1, D_MODEL), layer),
        ],
        out_specs=row,
        out_shape=jax.ShapeDtypeStruct((m, D_MODEL), F32),
        compiler_params=_cparams(1),
        name="ffn",
    )(x, wts["w_ff1"], wts["w_ff3"], wts["w_ff2"], wts["ln2_g"], wts["ln2_b"])


def _rope_tables(pos, tm):
    half = HEAD_DIM // 2
    inv = ROPE_THETA ** (-jnp.arange(half, dtype=F32) / half)
    ang = pos.astype(F32)[:, None] * inv[None, :]
    cos, sin = jnp.cos(ang), jnp.sin(ang)
    cos_t = jnp.concatenate([cos, cos, cos, cos], axis=1)
    sin_t = jnp.concatenate([-sin, sin, -sin, sin], axis=1)
    reps = max(1, tm // pos.shape[0])
    return jnp.tile(cos_t, (reps, 1)), jnp.tile(sin_t, (reps, 1))


def _prep_weights(w_in, b_fgate, b_gate, diff_norm_g, w_br_a, w_br_b, w_br_c, w_o, ln1_g, ln1_b, ln2_g, ln2_b,
                  w_ff1, w_ff3, w_ff2, lam_q1, lam_k1, lam_q2, lam_k2):
    depth = w_in.shape[0]
    offs = np.cumsum((0,) + SPLIT_SIZES)
    col = lambda i: w_in[:, :, offs[i]:offs[i + 1]]
    aq, ak, av, af, bq, bk, bv, cq, ck, cv, iq, ik, iw, gl = (col(i) for i in range(len(SPLIT_SIZES)))
    qs = HEAD_DIM ** -0.5 * LOG2E
    misc_pad = jnp.zeros((depth, D_MODEL, W_HEADS - D_IDX - H_FOX - H_IDX), w_in.dtype)
    w_slab = jnp.concatenate([aq * qs, ak, av, bq * qs, bk, bv, cq * qs, ck, cv, iq * (D_IDX ** -0.5),
                              ik, af, iw, misc_pad], axis=2).astype(BF16)
    bf_row = jnp.zeros((depth, 1, LANES), F32).at[:, 0, MISC_LOGF:MISC_LOGF + H_FOX].set(b_fgate.astype(F32))
    return {
        "w_slab": w_slab,
        "bf_row": bf_row,
        "w_gl": gl.astype(BF16),
        "b_gate": b_gate.astype(F32),
        "w_br_a": w_br_a.astype(BF16), "w_br_b": w_br_b.astype(BF16), "w_br_c": w_br_c.astype(BF16),
        "w_o": w_o.astype(BF16),
        "ln1_g": ln1_g[:, None, :].astype(F32), "ln1_b": ln1_b[:, None, :].astype(F32),
        "ln2_g": ln2_g[:, None, :].astype(F32), "ln2_b": ln2_b[:, None, :].astype(F32),
        "w_ff1": w_ff1.astype(BF16), "w_ff3": w_ff3.astype(BF16), "w_ff2": w_ff2.astype(BF16),
        "lam4": jnp.stack([lam_q1, lam_k1, lam_q2, lam_k2], axis=1).astype(F32),
        "diff_g": diff_norm_g[:, None, :].astype(F32),
    }


def _layer_group(x, wts, layer, tabs, bsz, t, caches):
    cos_t, sin_t = tabs
    proj = _inproj(x, wts["w_slab"], layer, cos_t, sin_t, wts["bf_row"], t, caches is None)
    slab, misc = proj[:2]
    logf = misc[:, MISC_LOGF:MISC_LOGF + H_FOX].reshape(bsz, t, H_FOX)
    lf_all = misc.reshape(bsz, t, LANES)
    past = 0
    if caches is not None:
        past = caches["a_logf"].shape[2]
        past_lf = jnp.pad(caches["a_logf"][layer].astype(F32),
                          ((0, 0), (0, 0), (MISC_LOGF, LANES - MISC_LOGF - H_FOX)))
        lf_all = jnp.concatenate([past_lf, lf_all], axis=1)
    n_keys = past + t
    cum = _cumsum_time(lf_all)[:, :, MISC_LOGF:MISC_LOGF + H_FOX]
    npair = H_FOX // 2
    cum_rows = cum.transpose(0, 2, 1).reshape(bsz, npair, 2, n_keys)
    cum_cols = cum.reshape(bsz, n_keys, npair, 2).transpose(0, 2, 1, 3)
    lam_init = 0.8 - 0.6 * math.exp(-0.3 * layer)
    diff_extra = (wts["lam4"][layer], wts["diff_g"][layer])
    k_sel = min(TOPK_MAX, n_keys // 4)
    if caches is None:
        slab3 = slab.reshape(bsz, t, SLAB_W)
        vt_slab = proj[2]
        ya = _pair_attn_prompt("fox", slab, slab3, vt_slab, G_AQ, G_AK, 0, (cum_rows, cum_cols), lam_init)
        yb = _pair_attn_prompt("diff", slab, slab3, vt_slab, G_BQ, G_BK, 1, diff_extra, lam_init)
        yc = _dsa_prompt(slab, slab3, vt_slab, misc, k_sel)
    else:
        cq4, ck4 = cum_cols[:, :, past:], cum_rows
        ya = _pair_attn_sample("fox", slab, t, caches["a_k"], caches["a_v"], layer, G_AQ, G_AK, G_AV,
                               (cq4, ck4), lam_init)
        yb = _pair_attn_sample("diff", slab, t, caches["b_k"], caches["b_v"], layer, G_BQ, G_BK, G_BV,
                               diff_extra, lam_init)
        yc = _dsa_sample(slab, misc, t, caches["c_idx"], caches["c_k"], caches["c_v"], layer, k_sel)
    x = _merge(x, ya, yb, yc, wts, layer)
    x = _ffn(x, wts, layer)
    if caches is None:
        akt, avt, bkt, bv, ckt, cvt, misct = proj[3:]
        new_rows = (akt, avt, misct[:, MISC_LOGF:MISC_LOGF + H_FOX], bkt, bv, ckt, cvt, misct[:, :D_IDX])
    else:
        ak, av, bk, bv, ck, cv = proj[2:]
        new_rows = (
            ak.reshape(bsz, t, H_FOX, HEAD_DIM), av.reshape(bsz, t, H_FOX, HEAD_DIM), logf,
            bk.reshape(bsz, t, H_DIFF, 2, HEAD_DIM), bv.reshape(bsz, t, H_DIFF, 2 * HEAD_DIM),
            ck.reshape(bsz, t, H_DSA, HEAD_DIM), cv.reshape(bsz, t, H_DSA, HEAD_DIM),
            misc[:, :D_IDX].reshape(bsz, t, D_IDX),
        )
    return x, new_rows


def _prompt_outputs(rows, bsz, t):
    depth = len(rows)
    st = lambda i: jnp.stack([r[i] for r in rows])
    heads = lambda a, *dims: jnp.moveaxis(a.reshape((depth, bsz) + dims + (t,)), -1, 2)
    return [
        heads(st(0), H_FOX, HEAD_DIM), heads(st(1), H_FOX, HEAD_DIM), heads(st(2), H_FOX),
        heads(st(3), H_DIFF, 2, HEAD_DIM), st(4).reshape(depth, bsz, t, H_DIFF, 2 * HEAD_DIM),
        heads(st(5), H_DSA, HEAD_DIM), heads(st(6), H_DSA, HEAD_DIM), heads(st(7), D_IDX),
    ]


def kernel(x_prompt, x_sample, cache_a_k, cache_a_v, cache_a_logf, cache_b_k, cache_b_v, cache_c_k, cache_c_v, cache_c_idx, w_in, b_fgate, b_gate, lam_q1, lam_k1, lam_q2, lam_k2, diff_norm_g, w_br_a, w_br_b, w_br_c, w_o, ln1_g, ln1_b, ln2_g, ln2_b, w_ff1, w_ff3, w_ff2):
    depth = w_in.shape[0]
    bp, tp, _ = x_prompt.shape
    bs, ts, _ = x_sample.shape
    past = cache_a_k.shape[2]
    wts = _prep_weights(w_in, b_fgate, b_gate, diff_norm_g, w_br_a, w_br_b, w_br_c, w_o, ln1_g, ln1_b, ln2_g, ln2_b,
                        w_ff1, w_ff3, w_ff2, lam_q1, lam_k1, lam_q2, lam_k2)
    tabs_p = _rope_tables(jnp.arange(tp, dtype=I32), min(TM_DENSE, bp * tp))
    tabs_s = _rope_tables(past + jnp.arange(ts, dtype=I32), min(TM_DENSE, bs * ts))
    flat = lambda c: c.reshape(c.shape[0], c.shape[1], c.shape[2], -1)
    fmaj = lambda c: jnp.moveaxis(flat(c), 2, 3)
    caches = {
        "a_k": fmaj(cache_a_k), "a_v": fmaj(cache_a_v), "a_logf": cache_a_logf,
        "b_k": fmaj(cache_b_k), "b_v": flat(cache_b_v),
        "c_k": fmaj(cache_c_k), "c_v": fmaj(cache_c_v), "c_idx": fmaj(cache_c_idx),
    }
    yp = x_prompt.reshape(bp * tp, D_MODEL)
    ys = x_sample.reshape(bs * ts, D_MODEL)
    rows_p, rows_s = [], []
    for layer in range(depth):
        yp, rp = _layer_group(yp, wts, layer, tabs_p, bp, tp, None)
        ys, rs = _layer_group(ys, wts, layer, tabs_s, bs, ts, caches)
        rows_p.append(rp)
        rows_s.append(rs)
    outs_p = _prompt_outputs(rows_p, bp, tp)
    outs_s = [jnp.stack([r[i] for r in rows_s]) for i in range(8)]
    return (yp.reshape(bp, tp, D_MODEL), ys.reshape(bs, ts, D_MODEL), *outs_p, *outs_s)
```

```python
import functools
import math

import jax
import jax.numpy as jnp
import numpy as np
from jax import lax
from jax.experimental import pallas as pl
from jax.experimental.pallas import tpu as pltpu

F32 = jnp.float32
BF16 = jnp.bfloat16
I32 = jnp.int32

D_MODEL = 1024
HEAD_DIM = 64
H_FOX = 8
H_DIFF = 4
H_DSA = 8
H_IDX = 8
D_IDX = 64
CHUNK = 64
CHUNK_SHIFT = 6
TOPK_MAX = 256
ROPE_THETA = 10000.0
N_BRANCH = 3
LN_EPS = 1e-5
MODEL_DEPTH = 4
DEEPNORM_ALPHA = (2 * MODEL_DEPTH) ** 0.25
LOG2E = math.log2(math.e)
W_HEADS = 512
SPLIT_SIZES = (W_HEADS, W_HEADS, W_HEADS, H_FOX, W_HEADS, W_HEADS, W_HEADS, W_HEADS, W_HEADS, W_HEADS,
               H_IDX * D_IDX, D_IDX, H_IDX, N_BRANCH * D_MODEL)

LANES = 128
N_GROUPS = 11
SLAB_W = N_GROUPS * W_HEADS
G_AQ, G_AK, G_AV, G_BQ, G_BK, G_BV, G_CQ, G_CK, G_CV, G_IQ, G_MISC = range(N_GROUPS)
MISC_LOGF = 64
MISC_IW = 72
VMEM_LIMIT = 56 * 1024 * 1024
NEG_INF = float("-inf")
INT_MIN = np.int32(-2 ** 31)
INT_MAX_MASK = np.int32(2 ** 31 - 1)

TM_DENSE = 512
TQ_ATTN = 512
TQ_DSA = 512
SEG_DSA = 512
CH_DSA = 512


def _cparams(n_axes):
    return pltpu.CompilerParams(dimension_semantics=("arbitrary",) * n_axes, vmem_limit_bytes=VMEM_LIMIT)


def _lane_iota(n=LANES):
    return lax.broadcasted_iota(I32, (1, n), 1)


def _inproj_kernel(x_ref, w_ref, wn_ref, cos_ref, sin_ref, bf_ref, s16_ref, misc_ref, *rest, feature_major):
    if feature_major:
        vt_ref, akt_ref, avt_ref, bkt_ref, bv_ref, ckt_ref, cvt_ref, misct_ref, xb_ref, h_ref = rest
        t_out = {G_AK: akt_ref, G_AV: avt_ref, G_BK: bkt_ref, G_CK: ckt_ref, G_CV: cvt_ref}
        r_out = {G_BV: bv_ref}
    else:
        ak_ref, av_ref, bk_ref, bv_ref, ck_ref, cv_ref, xb_ref, h_ref = rest
        t_out = {}
        r_out = {G_AK: ak_ref, G_AV: av_ref, G_BK: bk_ref, G_BV: bv_ref, G_CK: ck_ref, G_CV: cv_ref}
    j = pl.program_id(1)
    slot = j & 1

    def project(w):
        return jnp.dot(xb_ref[...], w[...], preferred_element_type=F32)

    @pl.when(j == 0)
    def _():
        xb_ref[...] = x_ref[...].astype(BF16)
        h_ref[0] = project(w_ref)

    lane = _lane_iota()
    sel_up = (lane & (HEAD_DIM - 1)) < HEAD_DIM // 2

    def rope128(xb):
        up = pltpu.roll(xb, LANES - HEAD_DIM // 2, 1)
        dn = pltpu.roll(xb, HEAD_DIM // 2, 1)
        return xb * cos_ref[...] + jnp.where(sel_up, up, dn) * sin_ref[...]

    def finish(g, h):
        if g == G_MISC:
            hb = h[:, :LANES]
            roped = rope128(hb)
            z = hb + bf_ref[...]
            logf = jnp.minimum(z, 0.0) - jnp.log1p(jnp.exp(-jnp.abs(z)))
            misc = jnp.where(lane < MISC_LOGF, roped, jnp.where(lane < MISC_IW, logf, hb))
            misc_ref[...] = misc
            if feature_major:
                misct_ref[0] = misc.T
            ikd = jnp.where(lane < D_IDX, roped, pltpu.roll(roped, D_IDX, 1))
            s16_ref[:, :LANES] = ikd.astype(BF16)
            s16_ref[:, LANES:] = jnp.zeros((s16_ref.shape[0], W_HEADS - LANES), BF16)
            return
        if g in (G_BQ, G_BK, G_CQ, G_CK, G_IQ):
            val = jnp.concatenate([rope128(h[:, c * LANES:(c + 1) * LANES]) for c in range(W_HEADS // LANES)], axis=1)
        else:
            val = h
        s16_ref[...] = val.astype(BF16)
        if g in r_out:
            r_out[g][...] = val
        if feature_major and (g in t_out or g == G_BV):
            val_t = val.T
            if g in t_out:
                t_out[g][0] = val_t
            if g in (G_AV, G_BV, G_CV):
                vt_ref[0] = val_t.astype(BF16)

    for g in range(N_GROUPS):
        @pl.when(j == g)
        def _(g=g):
            h = h_ref[g & 1]
            if g + 1 < N_GROUPS:
                h_ref[(g + 1) & 1] = project(wn_ref)
            finish(g, h)


def _inproj(x, w_all, layer, cos_t, sin_t, bf_row, seq, feature_major):
    m = x.shape[0]
    tm = min(TM_DENSE, m)
    n_tab = cos_t.shape[0] // tm
    tab_map = lambda i, j: (i % n_tab, 0)
    rows32 = (pl.BlockSpec((tm, W_HEADS), lambda i, j: (i, 0)), jax.ShapeDtypeStruct((m, W_HEADS), F32))
    out_specs = [pl.BlockSpec((tm, W_HEADS), lambda i, j: (i, j)), pl.BlockSpec((tm, LANES), lambda i, j: (i, 0))]
    out_shape = [jax.ShapeDtypeStruct((m, SLAB_W), BF16), jax.ShapeDtypeStruct((m, LANES), F32)]
    if feature_major:
        nt = seq // tm
        bsz = m // seq
        cols32 = (pl.BlockSpec((1, W_HEADS, tm), lambda i, j: (i // nt, 0, i % nt)),
                  jax.ShapeDtypeStruct((bsz, W_HEADS, seq), F32))
        outs = [(pl.BlockSpec((1, W_HEADS, tm), lambda i, j: (i // nt, jnp.minimum(j // 3, 2), i % nt)),
                 jax.ShapeDtypeStruct((bsz, 3 * W_HEADS, seq), BF16)),
                cols32, cols32, cols32, rows32, cols32, cols32,
                (pl.BlockSpec((1, LANES, tm), lambda i, j: (i // nt, 0, i % nt)),
                 jax.ShapeDtypeStruct((bsz, LANES, seq), F32))]
    else:
        outs = [rows32] * 6
    out_specs += [o[0] for o in outs]
    out_shape += [o[1] for o in outs]
    return pl.pallas_call(
        functools.partial(_inproj_kernel, feature_major=feature_major),
        grid=(m // tm, N_GROUPS),
        in_specs=[
            pl.BlockSpec((tm, D_MODEL), lambda i, j: (i, 0)),
            pl.BlockSpec((None, D_MODEL, W_HEADS), lambda i, j: (layer, 0, 0)),
            pl.BlockSpec((None, D_MODEL, W_HEADS), lambda i, j: (layer, 0, jnp.minimum(j + 1, N_GROUPS - 1))),
            pl.BlockSpec((tm, LANES), tab_map),
            pl.BlockSpec((tm, LANES), tab_map),
            pl.BlockSpec((None, 1, LANES), lambda i, j: (layer, 0, 0)),
        ],
        out_specs=out_specs,
        out_shape=out_shape,
        scratch_shapes=[pltpu.VMEM((tm, D_MODEL), BF16), pltpu.VMEM((2, tm, W_HEADS), F32)],
        compiler_params=_cparams(2),
        name="inproj",
    )(x, w_all, w_all, cos_t, sin_t, bf_row)


def _split3(x):
    hi = x.astype(BF16)
    r1 = x - hi.astype(F32)
    mid = r1.astype(BF16)
    lo = (r1 - mid.astype(F32)).astype(BF16)
    return hi, mid, lo


def _cumsum_kernel(x_ref, o_ref, carry_ref):
    t = pl.program_id(1)

    @pl.when(t == 0)
    def _():
        carry_ref[...] = jnp.zeros_like(carry_ref)

    x = x_ref[0]
    tl = x.shape[0]
    r = lax.broadcasted_iota(I32, (tl, tl), 0)
    c = lax.broadcasted_iota(I32, (tl, tl), 1)
    tri = jnp.where(c <= r, 1.0, 0.0).astype(BF16)
    hi, mid, lo = _split3(x)
    dot = lambda a: jnp.dot(tri, a, preferred_element_type=F32)
    cs = (dot(lo) + dot(mid)) + dot(hi) + carry_ref[...]
    o_ref[0] = cs * LOG2E
    carry_ref[...] = cs[tl - 1:tl, :]


def _cumsum_time(x):
    b, l, h = x.shape
    tl = 512 if l % 512 == 0 else l
    return pl.pallas_call(
        _cumsum_kernel,
        grid=(b, l // tl),
        in_specs=[pl.BlockSpec((1, tl, h), lambda i, t: (i, t, 0))],
        out_specs=pl.BlockSpec((1, tl, h), lambda i, t: (i, t, 0)),
        out_shape=jax.ShapeDtypeStruct((b, l, h), F32),
        scratch_shapes=[pltpu.VMEM((1, h), F32)],
        compiler_params=_cparams(2),
        name="cumsum",
    )(x)


def _split_pair(q):
    lane = _lane_iota()
    zero = jnp.zeros_like(q)
    return jnp.where(lane < HEAD_DIM, q, zero), jnp.where(lane >= HEAD_DIM, q, zero)


def _qk(q, k):
    return lax.dot_general(q, k, (((1,), (1,)), ((), ())), preferred_element_type=F32)


def _online_update(s, v, m_old, l_old, acc_old):
    m_new = jnp.maximum(m_old, jnp.max(s, axis=1, keepdims=True))
    m_safe = jnp.where(m_new == NEG_INF, 0.0, m_new)
    alpha = jnp.exp2(m_old - m_safe)
    p = jnp.exp2(s - m_safe)
    l_new = alpha * l_old + jnp.sum(p, axis=1, keepdims=True)
    acc_new = alpha * acc_old + jnp.dot(p.astype(BF16), v, preferred_element_type=F32)
    return m_new, l_new, acc_new


def _causal_mask(kind, rows, cols, row0, col0):
    r = row0 + lax.broadcasted_iota(I32, (rows, 1), 0)
    c = col0 + lax.broadcasted_iota(I32, (1, cols), 1)
    if kind == "fox":
        return c <= r
    return (c >> CHUNK_SHIFT) <= (r >> CHUNK_SHIFT)


def _diff_lambda(lam_ref, lam_init):
    lp = lam_ref[...]
    s1 = jnp.sum(lp[0:1] * lp[1:2], axis=1, keepdims=True)
    s2 = jnp.sum(lp[2:3] * lp[3:4], axis=1, keepdims=True)
    return jnp.exp(s1) - jnp.exp(s2) + lam_init


def _pair_epilogue(kind, o0, o1, lam_ref, g_ref, lam_init):
    if kind == "fox":
        return jnp.where(_lane_iota() < HEAD_DIM, o0, o1)
    o = o0 - _diff_lambda(lam_ref, lam_init) * o1
    o = o * lax.rsqrt(jnp.mean(jnp.square(o), axis=1, keepdims=True) + LN_EPS) * g_ref[...]
    return o * (1.0 - lam_init)


SUM_ROWS = 16


def _with_sum_rows(vt):
    return jnp.concatenate([vt, jnp.ones((SUM_ROWS, vt.shape[1]), vt.dtype)], axis=0)


def _online_update_t(st, vt_aug, m_old, acc_old):
    m_new = jnp.maximum(m_old, jnp.max(st, axis=0, keepdims=True))
    m_safe = jnp.where(m_new == NEG_INF, 0.0, m_new)
    alpha = jnp.exp2(m_old - m_safe)
    pt = jnp.exp2(st - m_safe).astype(BF16)
    acc_new = alpha * acc_old + jnp.dot(vt_aug, pt, preferred_element_type=F32)
    return m_new, acc_new


def _normalised(acc):
    return acc[:LANES] * (1.0 / acc[LANES:LANES + 1])


def _causal_mask_t(kind, keys, queries):
    kp = lax.broadcasted_iota(I32, (keys, 1), 0)
    qp = lax.broadcasted_iota(I32, (1, queries), 1)
    if kind == "fox":
        return kp <= qp
    return (kp >> CHUNK_SHIFT) <= (qp >> CHUNK_SHIFT)


def _init_stats(m_ref, acc_ref):
    m_ref[...] = jnp.full(m_ref.shape, NEG_INF, F32)
    acc_ref[...] = jnp.zeros(acc_ref.shape, F32)


def _pair_attn_prompt_kernel(*refs, kind, tq, lam_init):
    if kind == "fox":
        q_ref, k_ref, vt_ref, cq_ref, ck_ref, o_ref, m_ref, acc_ref, st_ref = refs
        lam_ref = g_ref = None
    else:
        q_ref, k_ref, vt_ref, lam_ref, g_ref, o_ref, m_ref, acc_ref, st_ref = refs
    qi = pl.program_id(2)
    qs = _split_pair(q_ref[...])
    _init_stats(m_ref, acc_ref)

    def scores(j, slot):
        start = pl.multiple_of(j * tq, tq)
        k = k_ref[0, pl.ds(start, tq), :]
        for h in range(2):
            st = _qk(k, qs[h])
            if kind == "fox":
                st = st + cq_ref[0, 0, h:h + 1, :] - ck_ref[0, 0, pl.ds(start, tq), h:h + 1]
            st_ref[slot, h] = st

    def consume(j, slot, masked):
        start = pl.multiple_of(j * tq, tq)
        vt = _with_sum_rows(vt_ref[0, :, pl.ds(start, tq)])
        for h in range(2):
            st = st_ref[slot, h]
            if masked:
                st = jnp.where(_causal_mask_t(kind, tq, tq), st, NEG_INF)
            m_ref[h], acc_ref[h] = _online_update_t(st, vt, m_ref[h], acc_ref[h])

    scores(0, 0)

    def pair(jj, carry):
        j = 2 * jj
        consume(j, 0, False)
        scores(j + 1, 1)
        consume(j + 1, 1, False)
        scores(j + 2, 0)
        return carry

    lax.fori_loop(0, qi // 2, pair, 0)

    @pl.when(qi % 2 == 1)
    def _():
        consume(qi - 1, 0, False)
        scores(qi, 1)
        consume(qi, 1, True)

    @pl.when(qi % 2 == 0)
    def _():
        consume(qi, 0, True)

    o0 = _normalised(acc_ref[0]).T
    o1 = _normalised(acc_ref[1]).T
    o_ref[...] = _pair_epilogue(kind, o0, o1, lam_ref, g_ref, lam_init).astype(BF16)


def _pair_attn_prompt(kind, slab2, slab3, vt_slab, qg, kg, vt_group, extra, lam_init):
    b, t, _ = slab3.shape
    tq = min(TQ_ATTN, t)
    nq = t // tq
    npair = W_HEADS // LANES
    in_specs = [
        pl.BlockSpec((tq, LANES), lambda i, p, q: (i * nq + q, qg * npair + p)),
        pl.BlockSpec((1, t, LANES), lambda i, p, q: (i, 0, kg * npair + p)),
        pl.BlockSpec((1, LANES, t), lambda i, p, q: (i, vt_group * npair + p, 0)),
    ]
    if kind == "fox":
        in_specs += [
            pl.BlockSpec((1, 1, 2, tq), lambda i, p, q: (i, p, 0, q)),
            pl.BlockSpec((1, 1, t, 2), lambda i, p, q: (i, p, 0, 0)),
        ]
    else:
        in_specs += [
            pl.BlockSpec((4, HEAD_DIM), lambda i, p, q: (0, 0)),
            pl.BlockSpec((1, LANES), lambda i, p, q: (0, 0)),
        ]
    return pl.pallas_call(
        functools.partial(_pair_attn_prompt_kernel, kind=kind, tq=tq, lam_init=lam_init),
        grid=(b, npair, nq),
        in_specs=in_specs,
        out_specs=pl.BlockSpec((tq, LANES), lambda i, p, q: (i * nq + q, p)),
        out_shape=jax.ShapeDtypeStruct((b * t, W_HEADS), BF16),
        scratch_shapes=[pltpu.VMEM((2, 1, tq), F32), pltpu.VMEM((2, LANES + SUM_ROWS, tq), F32),
                        pltpu.VMEM((2, 2, tq, tq), F32)],
        compiler_params=_cparams(3),
        name=kind + "_prompt",
    )(slab2, slab3, vt_slab, *extra)


def _pair_attn_sample_kernel(*refs, kind, t, past, lam_init):
    if kind == "fox":
        q_ref, kp_ref, vp_ref, kn_ref, vn_ref, cq_ref, ck_ref, o_ref = refs
        lam_ref = g_ref = None
    else:
        q_ref, kp_ref, vp_ref, kn_ref, vn_ref, lam_ref, g_ref, o_ref = refs
    qs = _split_pair(q_ref[...])
    kp = kp_ref[...].T.astype(BF16)
    vp = (vp_ref[...].T if kind == "fox" else vp_ref[...]).astype(BF16)
    kn = kn_ref[...]
    vn = vn_ref[...]
    outs = []
    for h in range(2):
        m = jnp.full((t, 1), NEG_INF, F32)
        l = jnp.zeros((t, 1), F32)
        acc = jnp.zeros((t, LANES), F32)
        s = _qk(qs[h], kp)
        if kind == "fox":
            s = s + cq_ref[0, 0, :, h:h + 1] - ck_ref[0, 0, h:h + 1, :past]
        m, l, acc = _online_update(s, vp, m, l, acc)
        s = _qk(qs[h], kn)
        if kind == "fox":
            s = s + cq_ref[0, 0, :, h:h + 1] - ck_ref[0, 0, h:h + 1, past:past + t]
        s = jnp.where(_causal_mask(kind, t, t, past, past), s, NEG_INF)
        m, l, acc = _online_update(s, vn, m, l, acc)
        outs.append(acc * (1.0 / l))
    o_ref[...] = _pair_epilogue(kind, outs[0], outs[1], lam_ref, g_ref, lam_init).astype(BF16)


def _pair_attn_sample(kind, slab2, t, cache_k, cache_v, layer, qg, kg, vg, extra, lam_init):
    b = slab2.shape[0] // t
    past = cache_k.shape[3]
    npair = W_HEADS // LANES
    feature_major = pl.BlockSpec((None, None, LANES, past), lambda i, p: (layer, i, p, 0))
    time_major = pl.BlockSpec((None, None, past, LANES), lambda i, p: (layer, i, 0, p))
    in_specs = [
        pl.BlockSpec((t, LANES), lambda i, p: (i, qg * npair + p)),
        feature_major,
        feature_major if kind == "fox" else time_major,
        pl.BlockSpec((t, LANES), lambda i, p: (i, kg * npair + p)),
        pl.BlockSpec((t, LANES), lambda i, p: (i, vg * npair + p)),
    ]
    if kind == "fox":
        in_specs += [
            pl.BlockSpec((1, 1, t, 2), lambda i, p: (i, p, 0, 0)),
            pl.BlockSpec((1, 1, 2, past + t), lambda i, p: (i, p, 0, 0)),
        ]
    else:
        in_specs += [
            pl.BlockSpec((4, HEAD_DIM), lambda i, p: (0, 0)),
            pl.BlockSpec((1, LANES), lambda i, p: (0, 0)),
        ]
    return pl.pallas_call(
        functools.partial(_pair_attn_sample_kernel, kind=kind, t=t, past=past, lam_init=lam_init),
        grid=(b, npair),
        in_specs=in_specs,
        out_specs=pl.BlockSpec((t, LANES), lambda i, p: (i, p)),
        out_shape=jax.ShapeDtypeStruct((b * t, W_HEADS), BF16),
        compiler_params=_cparams(2),
        name=kind + "_sample",
    )(slab2, cache_k, cache_v, slab2, slab2, *extra)


def _dsa_core(iq_ref, iw, ikd_ref, cq_ref, ck_ref, cv_ref, o_ref, keys_ref, bias_ref, m_ref, l_ref, acc_ref,
              *, tq, n_l, ch, qpos0, n_keys, k_sel):
    nch = n_l // ch
    lane = _lane_iota()
    qchunk = (qpos0 + lax.broadcasted_iota(I32, (tq, 1), 0)) >> CHUNK_SHIFT
    w = [iw[:, MISC_IW + h:MISC_IW + h + 1] for h in range(H_IDX)]

    def score_chunk(c, carry):
        start = pl.multiple_of(c * ch, ch)
        ik = ikd_ref[pl.ds(start, ch), :]
        acc = jnp.zeros((tq, ch), F32)
        for hp in range(H_IDX // 2):
            pair = _split_pair(iq_ref[:, hp * LANES:(hp + 1) * LANES])
            for half in range(2):
                acc = acc + jnp.maximum(_qk(pair[half], ik), 0.0) * w[2 * hp + half]
        score = acc * (H_IDX ** -0.5) + 0.0
        bits = lax.bitcast_convert_type(score, I32)
        key = bits ^ ((bits >> 31) & INT_MAX_MASK)
        kpos = start + lax.broadcasted_iota(I32, (1, ch), 1)
        adm = ((kpos >> CHUNK_SHIFT) <= qchunk) & (kpos < n_keys)
        keys_ref[:, pl.ds(start, ch)] = jnp.where(adm, key, INT_MIN)
        return carry

    lax.fori_loop(0, nch, score_chunk, 0)

    def count(pred):
        def body(c, a):
            start = pl.multiple_of(c * ch, ch)
            hit = jnp.where(pred(keys_ref[:, pl.ds(start, ch)]), 1.0, 0.0)
            for t in range(ch // LANES):
                a = a + hit[:, t * LANES:(t + 1) * LANES]
            return a
        part = lax.fori_loop(0, nch, body, jnp.zeros((tq, LANES), F32))
        return jnp.sum(part, axis=1, keepdims=True)

    def bit_step(i, thr_u):
        cand_u = thr_u | lax.shift_left(jnp.int32(1), 31 - i)
        cand_s = cand_u ^ INT_MIN
        n_ge = count(lambda kk: kk >= cand_s)
        return jnp.where(n_ge >= k_sel, cand_u, thr_u)

    thr = lax.fori_loop(0, 32, bit_step, jnp.zeros((tq, 1), I32)) ^ INT_MIN
    need = k_sel - count(lambda kk: kk > thr)

    r128 = lax.broadcasted_iota(I32, (LANES, LANES), 0)
    c128 = lax.broadcasted_iota(I32, (LANES, LANES), 1)
    before = jnp.where(r128 < c128, 1.0, 0.0).astype(BF16)

    def select_block(t, seen):
        start = pl.multiple_of(t * LANES, LANES)
        kk = keys_ref[:, pl.ds(start, LANES)]
        eq = kk == thr
        eqf = jnp.where(eq, 1.0, 0.0)
        rank = seen + jnp.dot(eqf.astype(BF16), before, preferred_element_type=F32)
        sel = ((kk > thr) | (eq & (rank < need))) & (kk != INT_MIN)
        bias_ref[:, pl.ds(start, LANES)] = jnp.where(sel, 0.0, NEG_INF)
        return seen + jnp.sum(eqf, axis=1, keepdims=True)

    lax.fori_loop(0, n_l // LANES, select_block, jnp.zeros((tq, 1), F32))

    for p in range(H_DSA // 2):
        sl = slice(p * LANES, (p + 1) * LANES)
        qs = _split_pair(cq_ref[:, sl])
        m_ref[...] = jnp.full(m_ref.shape, NEG_INF, F32)
        l_ref[...] = jnp.zeros(l_ref.shape, F32)
        acc_ref[...] = jnp.zeros(acc_ref.shape, F32)

        def attend(c, carry):
            start = pl.multiple_of(c * ch, ch)
            k = ck_ref[pl.ds(start, ch), sl]
            v = cv_ref[pl.ds(start, ch), sl]
            bias = bias_ref[:, pl.ds(start, ch)]
            for h in range(2):
                s = _qk(qs[h], k) + bias
                m_ref[h], l_ref[h], acc_ref[h] = _online_update(s, v, m_ref[h], l_ref[h], acc_ref[h])
            return carry

        lax.fori_loop(0, nch, attend, 0)
        o0 = acc_ref[0] * (1.0 / l_ref[0])
        o1 = acc_ref[1] * (1.0 / l_ref[1])
        o_ref[:, sl] = jnp.where(lane < HEAD_DIM, o0, o1).astype(BF16)


def _dsa_scratch(tq, n_l):
    return [pltpu.VMEM((tq, n_l), I32), pltpu.VMEM((tq, n_l), F32),
            pltpu.VMEM((2, tq, 1), F32), pltpu.VMEM((2, tq, 1), F32), pltpu.VMEM((2, tq, LANES), F32)]


def _dsa_core_t(iq_ref, iw_t, ikd_ref, cq_ref, ck_ref, cvt_ref, o_ref, keys_ref, bias_ref, m_ref, acc_ref, st_ref,
                *, tq, n_l, ch, qpos0, n_keys, k_sel):
    nch = n_l // ch
    qchunk = (qpos0 + _lane_iota(tq)) >> CHUNK_SHIFT
    w = [iw_t[MISC_IW + h:MISC_IW + h + 1, :] for h in range(H_IDX)]
    iqs = []
    for hp in range(H_IDX // 2):
        iqs.extend(_split_pair(iq_ref[:, hp * LANES:(hp + 1) * LANES]))

    def score_chunk(c, carry):
        start = pl.multiple_of(c * ch, ch)
        ik = ikd_ref[pl.ds(start, ch), :]
        acc = jnp.zeros((ch, tq), F32)
        for h in range(H_IDX):
            acc = acc + jnp.maximum(_qk(ik, iqs[h]), 0.0) * w[h]
        score = acc * (H_IDX ** -0.5) + 0.0
        bits = lax.bitcast_convert_type(score, I32)
        key = bits ^ ((bits >> 31) & INT_MAX_MASK)
        kpos = start + lax.broadcasted_iota(I32, (ch, 1), 0)
        adm = ((kpos >> CHUNK_SHIFT) <= qchunk) & (kpos < n_keys)
        keys_ref[pl.ds(start, ch), :] = jnp.where(adm, key, INT_MIN)
        return carry

    lax.fori_loop(0, nch, score_chunk, 0)

    n_acc = 4

    def count(pred):
        def body(c, accs):
            start = pl.multiple_of(c * ch, ch)
            kk = keys_ref[pl.ds(start, ch), :]
            accs = list(accs)
            for g in range(ch // 8):
                a = accs[g % n_acc]
                accs[g % n_acc] = jnp.where(pred(kk[g * 8:(g + 1) * 8, :]), a + 1.0, a)
            return tuple(accs)
        accs = lax.fori_loop(0, nch, body, tuple(jnp.zeros((8, tq), F32) for _ in range(n_acc)))
        return jnp.sum((accs[0] + accs[1]) + (accs[2] + accs[3]), axis=0, keepdims=True)

    def bit_step(i, thr_u):
        cand_u = thr_u | lax.shift_left(jnp.int32(1), 31 - i)
        cand_s = cand_u ^ INT_MIN
        n_ge = count(lambda kk: kk >= cand_s)
        return jnp.where(n_ge >= k_sel, cand_u, thr_u)

    thr = lax.fori_loop(0, 32, bit_step, jnp.zeros((1, tq), I32)) ^ INT_MIN
    need = k_sel - count(lambda kk: kk > thr)

    sb = 2 * LANES
    r_i = lax.broadcasted_iota(I32, (sb, sb), 0)
    c_i = lax.broadcasted_iota(I32, (sb, sb), 1)
    before = jnp.where(c_i < r_i, 1.0, 0.0).astype(BF16)

    def select_block(t, seen):
        start = pl.multiple_of(t * sb, sb)
        kk = keys_ref[pl.ds(start, sb), :]
        eq = kk == thr
        eqf = jnp.where(eq, 1.0, 0.0)
        rank = seen + jnp.dot(before, eqf.astype(BF16), preferred_element_type=F32)
        sel = ((kk > thr) | (eq & (rank < need))) & (kk != INT_MIN)
        bias_ref[pl.ds(start, sb), :] = jnp.where(sel, 0.0, NEG_INF)
        return seen + jnp.sum(eqf, axis=0, keepdims=True)

    def select_all_equal(c, carry):
        start = pl.multiple_of(c * ch, ch)
        kk = keys_ref[pl.ds(start, ch), :]
        bias_ref[pl.ds(start, ch), :] = jnp.where((kk >= thr) & (kk != INT_MIN), 0.0, NEG_INF)
        return carry

    n_eq = count(lambda kk: kk == thr)
    partial_ties = jnp.max(jnp.where(n_eq > need, 1, 0)) > 0

    @pl.when(partial_ties)
    def _():
        lax.fori_loop(0, n_l // sb, select_block, jnp.zeros((1, tq), F32))

    @pl.when(jnp.logical_not(partial_ties))
    def _():
        lax.fori_loop(0, nch, select_all_equal, 0)

    head0_rows = lax.broadcasted_iota(I32, (LANES, 1), 0) < HEAD_DIM
    for p in range(H_DSA // 2):
        sl = slice(p * LANES, (p + 1) * LANES)
        qs = _split_pair(cq_ref[:, sl])
        _init_stats(m_ref, acc_ref)

        def scores(c, slot):
            start = pl.multiple_of(c * ch, ch)
            k = ck_ref[pl.ds(start, ch), sl]
            bias = bias_ref[pl.ds(start, ch), :]
            for h in range(2):
                st_ref[slot, h] = _qk(k, qs[h]) + bias

        def consume(c, slot):
            start = pl.multiple_of(c * ch, ch)
            vt = _with_sum_rows(cvt_ref[sl, pl.ds(start, ch)])
            for h in range(2):
                m_ref[h], acc_ref[h] = _online_update_t(st_ref[slot, h], vt, m_ref[h], acc_ref[h])

        scores(0, 0)

        def attend_pair(cc, carry):
            c = 2 * cc
            consume(c, 0)
            scores(c + 1, 1)
            consume(c + 1, 1)
            scores(c + 2, 0)
            return carry

        lax.fori_loop(0, (nch - 1) // 2, attend_pair, 0)
        if nch % 2 == 0:
            consume(nch - 2, 0)
            scores(nch - 1, 1)
            consume(nch - 1, 1)
        else:
            consume(nch - 1, 0)
        ot = jnp.where(head0_rows, _normalised(acc_ref[0]), _normalised(acc_ref[1]))
        o_ref[:, sl] = ot.T.astype(BF16)


def _dsa_prompt_kernel(iq_ref, misc_ref, ikd_ref, cq_ref, ck_ref, cvt_ref, o_ref, *scratch, tq, n_l, q0, n_keys, k_sel):
    qpos0 = q0 + pl.program_id(1) * tq
    _dsa_core_t(iq_ref, misc_ref[...].T, ikd_ref.at[0], cq_ref, ck_ref.at[0], cvt_ref.at[0], o_ref, *scratch,
                tq=tq, n_l=n_l, ch=min(CH_DSA, n_l), qpos0=qpos0, n_keys=n_keys, k_sel=k_sel)


def _dsa_prompt_segment(slab2, slab3, vt_slab, misc, seg, seg_len, k_sel):
    b, t, _ = slab3.shape
    tq = min(TQ_DSA, seg_len)
    nqs = seg_len // tq
    nq = t // tq
    n_l = (seg + 1) * seg_len
    row = lambda i, q: i * nq + seg * nqs + q
    npair = W_HEADS // LANES
    return pl.pallas_call(
        functools.partial(_dsa_prompt_kernel, tq=tq, n_l=n_l, q0=seg * seg_len, n_keys=t, k_sel=k_sel),
        grid=(b, nqs),
        in_specs=[
            pl.BlockSpec((tq, W_HEADS), lambda i, q: (row(i, q), G_IQ)),
            pl.BlockSpec((tq, LANES), lambda i, q: (row(i, q), 0)),
            pl.BlockSpec((1, n_l, LANES), lambda i, q: (i, 0, G_MISC * npair)),
            pl.BlockSpec((tq, W_HEADS), lambda i, q: (row(i, q), G_CQ)),
            pl.BlockSpec((1, n_l, W_HEADS), lambda i, q: (i, 0, G_CK)),
            pl.BlockSpec((1, W_HEADS, n_l), lambda i, q: (i, 2, 0)),
        ],
        out_specs=pl.BlockSpec((tq, W_HEADS), lambda i, q: (i * nqs + q, 0)),
        out_shape=jax.ShapeDtypeStruct((b * seg_len, W_HEADS), BF16),
        scratch_shapes=[pltpu.VMEM((n_l, tq), I32), pltpu.VMEM((n_l, tq), F32),
                        pltpu.VMEM((2, 1, tq), F32), pltpu.VMEM((2, LANES + SUM_ROWS, tq), F32),
                        pltpu.VMEM((2, 2, min(CH_DSA, n_l), tq), F32)],
        compiler_params=_cparams(2),
        name="dsa_prompt",
    )(slab2, misc, slab3, slab2, slab3, vt_slab)


def _dsa_prompt(slab2, slab3, vt_slab, misc, k_sel):
    b, t, _ = slab3.shape
    seg_len = min(SEG_DSA, t)
    segs = [_dsa_prompt_segment(slab2, slab3, vt_slab, misc, s, seg_len, k_sel) for s in range(t // seg_len)]
    y = jnp.stack([s.reshape(b, seg_len, W_HEADS) for s in segs], axis=1)
    return y.reshape(b * t, W_HEADS)


def _dsa_sample_kernel(iq_ref, misc_ref, ikp_ref, ckp_ref, cvp_ref, ikn_ref, cq_ref, ckn_ref, cvn_ref, o_ref,
                       ik_s, ck_s, cv_s, *scratch, t, past, n_l, ch, k_sel):
    pad = n_l - past - t
    ikp = ikp_ref[...]
    ik_s[:past, :] = jnp.concatenate([ikp, ikp], axis=0).T.astype(BF16)
    ik_s[past:past + t, :] = ikn_ref[...]
    ik_s[past + t:, :] = jnp.zeros((pad, LANES), BF16)
    ck_s[:past, :] = ckp_ref[...].T.astype(BF16)
    ck_s[past:past + t, :] = ckn_ref[...]
    ck_s[past + t:, :] = jnp.zeros((pad, W_HEADS), BF16)
    cv_s[:past, :] = cvp_ref[...].T.astype(BF16)
    cv_s[past:past + t, :] = cvn_ref[...]
    cv_s[past + t:, :] = jnp.zeros((pad, W_HEADS), BF16)
    _dsa_core(iq_ref, misc_ref[...], ik_s, cq_ref, ck_s, cv_s, o_ref, *scratch,
              tq=t, n_l=n_l, ch=ch, qpos0=past, n_keys=past + t, k_sel=k_sel)


def _dsa_sample(slab2, misc, t, ikd_past, cache_k, cache_v, layer, k_sel):
    b = slab2.shape[0] // t
    past = cache_k.shape[3]
    n_l = -(-(past + t) // LANES) * LANES
    ch = n_l
    npair = W_HEADS // LANES
    return pl.pallas_call(
        functools.partial(_dsa_sample_kernel, t=t, past=past, n_l=n_l, ch=ch, k_sel=k_sel),
        grid=(b,),
        in_specs=[
            pl.BlockSpec((t, W_HEADS), lambda i: (i, G_IQ)),
            pl.BlockSpec((t, LANES), lambda i: (i, 0)),
            pl.BlockSpec((None, None, D_IDX, past), lambda i: (layer, i, 0, 0)),
            pl.BlockSpec((None, None, W_HEADS, past), lambda i: (layer, i, 0, 0)),
            pl.BlockSpec((None, None, W_HEADS, past), lambda i: (layer, i, 0, 0)),
            pl.BlockSpec((t, LANES), lambda i: (i, G_MISC * npair)),
            pl.BlockSpec((t, W_HEADS), lambda i: (i, G_CQ)),
            pl.BlockSpec((t, W_HEADS), lambda i: (i, G_CK)),
            pl.BlockSpec((t, W_HEADS), lambda i: (i, G_CV)),
        ],
        out_specs=pl.BlockSpec((t, W_HEADS), lambda i: (i, 0)),
        out_shape=jax.ShapeDtypeStruct((b * t, W_HEADS), BF16),
        scratch_shapes=[pltpu.VMEM((n_l, LANES), BF16), pltpu.VMEM((n_l, W_HEADS), BF16),
                        pltpu.VMEM((n_l, W_HEADS), BF16)] + _dsa_scratch(t, n_l),
        compiler_params=_cparams(1),
        name="dsa_sample",
    )(slab2, misc, ikd_past, cache_k, cache_v, slab2, slab2, slab2, slab2)


def _layernorm(z, g_ref, b_ref):
    mu = jnp.mean(z, axis=1, keepdims=True)
    d = z - mu
    var = jnp.mean(jnp.square(d), axis=1, keepdims=True)
    return d * lax.rsqrt(var + LN_EPS) * g_ref[...] + b_ref[...]


def _merge_kernel(x_ref, ya_ref, yb_ref, yc_ref, wgl_ref, bg_ref, wa_ref, wb_ref, wc_ref, wo_ref, g_ref, b_ref, o_ref):
    x = x_ref[...]
    xb = x.astype(BF16)
    merged = None
    for i, (y_ref, w_ref) in enumerate(((ya_ref, wa_ref), (yb_ref, wb_ref), (yc_ref, wc_ref))):
        gl = jnp.dot(xb, wgl_ref[:, i * D_MODEL:(i + 1) * D_MODEL], preferred_element_type=F32)
        gate = jax.nn.sigmoid(gl + bg_ref[i:i + 1, :])
        term = gate * jnp.dot(y_ref[...], w_ref[...], preferred_element_type=F32)
        merged = term if merged is None else merged + term
    z = DEEPNORM_ALPHA * x + jnp.dot(merged.astype(BF16), wo_ref[...], preferred_element_type=F32)
    o_ref[...] = _layernorm(z, g_ref, b_ref)


def _resident(shape, layer):
    nd = len(shape)
    return pl.BlockSpec((None,) + tuple(shape), lambda i: (layer,) + (0,) * nd, pipeline_mode=pl.Buffered(1))


def _merge(x, ya, yb, yc, wts, layer):
    m = x.shape[0]
    tm = min(TM_DENSE, m)
    row = lambda w: pl.BlockSpec((tm, w), lambda i: (i, 0))
    return pl.pallas_call(
        _merge_kernel,
        grid=(m // tm,),
        in_specs=[
            row(D_MODEL), row(W_HEADS), row(W_HEADS), row(W_HEADS),
            _resident((D_MODEL, N_BRANCH * D_MODEL), layer),
            _resident((N_BRANCH, D_MODEL), layer),
            _resident((W_HEADS, D_MODEL), layer),
            _resident((W_HEADS, D_MODEL), layer),
            _resident((W_HEADS, D_MODEL), layer),
            _resident((D_MODEL, D_MODEL), layer),
            _resident((1, D_MODEL), layer),
            _resident((1, D_MODEL), layer),
        ],
        out_specs=row(D_MODEL),
        out_shape=jax.ShapeDtypeStruct((m, D_MODEL), F32),
        compiler_params=_cparams(1),
        name="merge",
    )(x, ya, yb, yc, wts["w_gl"], wts["b_gate"], wts["w_br_a"], wts["w_br_b"], wts["w_br_c"], wts["w_o"],
      wts["ln1_g"], wts["ln1_b"])


def _ffn_kernel(x_ref, w1_ref, w3_ref, w2_ref, g_ref, b_ref, o_ref, *, d_ff, fc):
    x = x_ref[...]
    xb = x.astype(BF16)
    acc = None
    for c in range(d_ff // fc):
        sl = slice(c * fc, (c + 1) * fc)
        h1 = jnp.dot(xb, w1_ref[:, sl], preferred_element_type=F32)
        h3 = jnp.dot(xb, w3_ref[:, sl], preferred_element_type=F32)
        u = (h1 * jax.nn.sigmoid(h1) * h3).astype(BF16)
        term = jnp.dot(u, w2_ref[sl, :], preferred_element_type=F32)
        acc = term if acc is None else acc + term
    o_ref[...] = _layernorm(DEEPNORM_ALPHA * x + acc, g_ref, b_ref)


def _ffn(x, wts, layer):
    m = x.shape[0]
    tm = min(TM_DENSE, m)
    d_ff = wts["w_ff1"].shape[2]
    row = pl.BlockSpec((tm, D_MODEL), lambda i: (i, 0))
    return pl.pallas_call(
        functools.partial(_ffn_kernel, d_ff=d_ff, fc=256),
        grid=(m // tm,),
        in_specs=[
            row,
            _resident((D_MODEL, d_ff), layer),
            _resident((D_MODEL, d_ff), layer),
            _resident((d_ff, D_MODEL), layer),
            _resident((1, D_MODEL), layer),
            _resident((1, D_MODEL), layer),
        ],
        out_specs=row,
        out_shape=jax.ShapeDtypeStruct((m, D_MODEL), F32),
        compiler_params=_cparams(1),
        name="ffn",
    )(x, wts["w_ff1"], wts["w_ff3"], wts["w_ff2"], wts["ln2_g"], wts["ln2_b"])


def _rope_tables(pos, tm):
    half = HEAD_DIM // 2
    inv = ROPE_THETA ** (-jnp.arange(half, dtype=F32) / half)
    ang = pos.astype(F32)[:, None] * inv[None, :]
    cos, sin = jnp.cos(ang), jnp.sin(ang)
    cos_t = jnp.concatenate([cos, cos, cos, cos], axis=1)
    sin_t = jnp.concatenate([-sin, sin, -sin, sin], axis=1)
    reps = max(1, tm // pos.shape[0])
    return jnp.tile(cos_t, (reps, 1)), jnp.tile(sin_t, (reps, 1))


def _prep_weights(w_in, b_fgate, b_gate, diff_norm_g, w_br_a, w_br_b, w_br_c, w_o, ln1_g, ln1_b, ln2_g, ln2_b,
                  w_ff1, w_ff3, w_ff2, lam_q1, lam_k1, lam_q2, lam_k2):
    depth = w_in.shape[0]
    offs = np.cumsum((0,) + SPLIT_SIZES)
    col = lambda i: w_in[:, :, offs[i]:offs[i + 1]]
    aq, ak, av, af, bq, bk, bv, cq, ck, cv, iq, ik, iw, gl = (col(i) for i in range(len(SPLIT_SIZES)))
    qs = HEAD_DIM ** -0.5 * LOG2E
    misc_pad = jnp.zeros((depth, D_MODEL, W_HEADS - D_IDX - H_FOX - H_IDX), w_in.dtype)
    w_slab = jnp.concatenate([aq * qs, ak, av, bq * qs, bk, bv, cq * qs, ck, cv, iq * (D_IDX ** -0.5),
                              ik, af, iw, misc_pad], axis=2).astype(BF16)
    bf_row = jnp.zeros((depth, 1, LANES), F32).at[:, 0, MISC_LOGF:MISC_LOGF + H_FOX].set(b_fgate.astype(F32))
    return {
        "w_slab": w_slab,
        "bf_row": bf_row,
        "w_gl": gl.astype(BF16),
        "b_gate": b_gate.astype(F32),
        "w_br_a": w_br_a.astype(BF16), "w_br_b": w_br_b.astype(BF16), "w_br_c": w_br_c.astype(BF16),
        "w_o": w_o.astype(BF16),
        "ln1_g": ln1_g[:, None, :].astype(F32), "ln1_b": ln1_b[:, None, :].astype(F32),
        "ln2_g": ln2_g[:, None, :].astype(F32), "ln2_b": ln2_b[:, None, :].astype(F32),
        "w_ff1": w_ff1.astype(BF16), "w_ff3": w_ff3.astype(BF16), "w_ff2": w_ff2.astype(BF16),
        "lam4": jnp.stack([lam_q1, lam_k1, lam_q2, lam_k2], axis=1).astype(F32),
        "diff_g": diff_norm_g[:, None, :].astype(F32),
    }


def _layer_group(x, wts, layer, tabs, bsz, t, caches):
    cos_t, sin_t = tabs
    proj = _inproj(x, wts["w_slab"], layer, cos_t, sin_t, wts["bf_row"], t, caches is None)
    slab, misc = proj[:2]
    logf = misc[:, MISC_LOGF:MISC_LOGF + H_FOX].reshape(bsz, t, H_FOX)
    lf_all = misc.reshape(bsz, t, LANES)
    past = 0
    if caches is not None:
        past = caches["a_logf"].shape[2]
        past_lf = jnp.pad(caches["a_logf"][layer].astype(F32),
                          ((0, 0), (0, 0), (MISC_LOGF, LANES - MISC_LOGF - H_FOX)))
        lf_all = jnp.concatenate([past_lf, lf_all], axis=1)
    n_keys = past + t
    cum = _cumsum_time(lf_all)[:, :, MISC_LOGF:MISC_LOGF + H_FOX]
    npair = H_FOX // 2
    cum_rows = cum.transpose(0, 2, 1).reshape(bsz, npair, 2, n_keys)
    cum_cols = cum.reshape(bsz, n_keys, npair, 2).transpose(0, 2, 1, 3)
    lam_init = 0.8 - 0.6 * math.exp(-0.3 * layer)
    diff_extra = (wts["lam4"][layer], wts["diff_g"][layer])
    k_sel = min(TOPK_MAX, n_keys // 4)
    if caches is None:
        slab3 = slab.reshape(bsz, t, SLAB_W)
        vt_slab = proj[2]
        ya = _pair_attn_prompt("fox", slab, slab3, vt_slab, G_AQ, G_AK, 0, (cum_rows, cum_cols), lam_init)
        yb = _pair_attn_prompt("diff", slab, slab3, vt_slab, G_BQ, G_BK, 1, diff_extra, lam_init)
        yc = _dsa_prompt(slab, slab3, vt_slab, misc, k_sel)
    else:
        cq4, ck4 = cum_cols[:, :, past:], cum_rows
        ya = _pair_attn_sample("fox", slab, t, caches["a_k"], caches["a_v"], layer, G_AQ, G_AK, G_AV,
                               (cq4, ck4), lam_init)
        yb = _pair_attn_sample("diff", slab, t, caches["b_k"], caches["b_v"], layer, G_BQ, G_BK, G_BV,
                               diff_extra, lam_init)
        yc = _dsa_sample(slab, misc, t, caches["c_idx"], caches["c_k"], caches["c_v"], layer, k_sel)
    x = _merge(x, ya, yb, yc, wts, layer)
    x = _ffn(x, wts, layer)
    if caches is None:
        akt, avt, bkt, bv, ckt, cvt, misct = proj[3:]
        new_rows = (akt, avt, misct[:, MISC_LOGF:MISC_LOGF + H_FOX], bkt, bv, ckt, cvt, misct[:, :D_IDX])
    else:
        ak, av, bk, bv, ck, cv = proj[2:]
        new_rows = (
            ak.reshape(bsz, t, H_FOX, HEAD_DIM), av.reshape(bsz, t, H_FOX, HEAD_DIM), logf,
            bk.reshape(bsz, t, H_DIFF, 2, HEAD_DIM), bv.reshape(bsz, t, H_DIFF, 2 * HEAD_DIM),
            ck.reshape(bsz, t, H_DSA, HEAD_DIM), cv.reshape(bsz, t, H_DSA, HEAD_DIM),
            misc[:, :D_IDX].reshape(bsz, t, D_IDX),
        )
    return x, new_rows


def _prompt_outputs(rows, bsz, t):
    depth = len(rows)
    st = lambda i: jnp.stack([r[i] for r in rows])
    heads = lambda a, *dims: jnp.moveaxis(a.reshape((depth, bsz) + dims + (t,)), -1, 2)
    return [
        heads(st(0), H_FOX, HEAD_DIM), heads(st(1), H_FOX, HEAD_DIM), heads(st(2), H_FOX),
        heads(st(3), H_DIFF, 2, HEAD_DIM), st(4).reshape(depth, bsz, t, H_DIFF, 2 * HEAD_DIM),
        heads(st(5), H_DSA, HEAD_DIM), heads(st(6), H_DSA, HEAD_DIM), heads(st(7), D_IDX),
    ]


def kernel(x_prompt, x_sample, cache_a_k, cache_a_v, cache_a_logf, cache_b_k, cache_b_v, cache_c_k, cache_c_v, cache_c_idx, w_in, b_fgate, b_gate, lam_q1, lam_k1, lam_q2, lam_k2, diff_norm_g, w_br_a, w_br_b, w_br_c, w_o, ln1_g, ln1_b, ln2_g, ln2_b, w_ff1, w_ff3, w_ff2):
    depth = w_in.shape[0]
    bp, tp, _ = x_prompt.shape
    bs, ts, _ = x_sample.shape
    past = cache_a_k.shape[2]
    wts = _prep_weights(w_in, b_fgate, b_gate, diff_norm_g, w_br_a, w_br_b, w_br_c, w_o, ln1_g, ln1_b, ln2_g, ln2_b,
                        w_ff1, w_ff3, w_ff2, lam_q1, lam_k1, lam_q2, lam_k2)
    tabs_p = _rope_tables(jnp.arange(tp, dtype=I32), min(TM_DENSE, bp * tp))
    tabs_s = _rope_tables(past + jnp.arange(ts, dtype=I32), min(TM_DENSE, bs * ts))
    flat = lambda c: c.reshape(c.shape[0], c.shape[1], c.shape[2], -1)
    fmaj = lambda c: jnp.moveaxis(flat(c), 2, 3)
    caches = {
        "a_k": fmaj(cache_a_k), "a_v": fmaj(cache_a_v), "a_logf": cache_a_logf,
        "b_k": fmaj(cache_b_k), "b_v": flat(cache_b_v),
        "c_k": fmaj(cache_c_k), "c_v": fmaj(cache_c_v), "c_idx": fmaj(cache_c_idx),
    }
    yp = x_prompt.reshape(bp * tp, D_MODEL)
    ys = x_sample.reshape(bs * ts, D_MODEL)
    rows_p, rows_s = [], []
    for layer in range(depth):
        yp, rp = _layer_group(yp, wts, layer, tabs_p, bp, tp, None)
        ys, rs = _layer_group(ys, wts, layer, tabs_s, bs, ts, caches)
        rows_p.append(rp)
        rows_s.append(rs)
    outs_p = _prompt_outputs(rows_p, bp, tp)
    outs_s = [jnp.stack([r[i] for r in rows_s]) for i in range(8)]
    return (yp.reshape(bp, tp, D_MODEL), ys.reshape(bs, ts, D_MODEL), *outs_p, *outs_s)
```

```python
import functools
import math

import jax
import jax.numpy as jnp
import numpy as np
from jax import lax
from jax.experimental import pallas as pl
from jax.experimental.pallas import tpu as pltpu

F32 = jnp.float32
BF16 = jnp.bfloat16
I32 = jnp.int32

D_MODEL = 1024
HEAD_DIM = 64
H_FOX = 8
H_DIFF = 4
H_DSA = 8
H_IDX = 8
D_IDX = 64
CHUNK = 64
CHUNK_SHIFT = 6
TOPK_MAX = 256
ROPE_THETA = 10000.0
N_BRANCH = 3
LN_EPS = 1e-5
MODEL_DEPTH = 4
DEEPNORM_ALPHA = (2 * MODEL_DEPTH) ** 0.25
LOG2E = math.log2(math.e)
W_HEADS = 512
SPLIT_SIZES = (W_HEADS, W_HEADS, W_HEADS, H_FOX, W_HEADS, W_HEADS, W_HEADS, W_HEADS, W_HEADS, W_HEADS,
               H_IDX * D_IDX, D_IDX, H_IDX, N_BRANCH * D_MODEL)

LANES = 128
N_GROUPS = 11
SLAB_W = N_GROUPS * W_HEADS
G_AQ, G_AK, G_AV, G_BQ, G_BK, G_BV, G_CQ, G_CK, G_CV, G_IQ, G_MISC = range(N_GROUPS)
MISC_LOGF = 64
MISC_IW = 72
VMEM_LIMIT = 56 * 1024 * 1024
NEG_INF = float("-inf")
INT_MIN = np.int32(-2 ** 31)
INT_MAX_MASK = np.int32(2 ** 31 - 1)

TM_DENSE = 512
TQ_ATTN = 512
TQ_DSA = 512
SEG_DSA = 512
CH_DSA = 512


def _cparams(n_axes):
    return pltpu.CompilerParams(dimension_semantics=("arbitrary",) * n_axes, vmem_limit_bytes=VMEM_LIMIT)


def _lane_iota(n=LANES):
    return lax.broadcasted_iota(I32, (1, n), 1)


def _inproj_kernel(x_ref, w_ref, wn_ref, cos_ref, sin_ref, bf_ref, s16_ref, misc_ref, *rest, feature_major):
    if feature_major:
        vt_ref, akt_ref, avt_ref, bkt_ref, bv_ref, ckt_ref, cvt_ref, misct_ref, xb_ref, h_ref = rest
        t_out = {G_AK: akt_ref, G_AV: avt_ref, G_BK: bkt_ref, G_CK: ckt_ref, G_CV: cvt_ref}
        r_out = {G_BV: bv_ref}
    else:
        ak_ref, av_ref, bk_ref, bv_ref, ck_ref, cv_ref, xb_ref, h_ref = rest
        t_out = {}
        r_out = {G_AK: ak_ref, G_AV: av_ref, G_BK: bk_ref, G_BV: bv_ref, G_CK: ck_ref, G_CV: cv_ref}
    j = pl.program_id(1)
    slot = j & 1

    def project(w):
        return jnp.dot(xb_ref[...], w[...], preferred_element_type=F32)

    @pl.when(j == 0)
    def _():
        xb_ref[...] = x_ref[...].astype(BF16)
        h_ref[0] = project(w_ref)

    lane = _lane_iota()
    sel_up = (lane & (HEAD_DIM - 1)) < HEAD_DIM // 2

    def rope128(xb):
        up = pltpu.roll(xb, LANES - HEAD_DIM // 2, 1)
        dn = pltpu.roll(xb, HEAD_DIM // 2, 1)
        return xb * cos_ref[...] + jnp.where(sel_up, up, dn) * sin_ref[...]

    def finish(g, h):
        if g == G_MISC:
            hb = h[:, :LANES]
            roped = rope128(hb)
            z = hb + bf_ref[...]
            logf = jnp.minimum(z, 0.0) - jnp.log1p(jnp.exp(-jnp.abs(z)))
            misc = jnp.where(lane < MISC_LOGF, roped, jnp.where(lane < MISC_IW, logf, hb))
            misc_ref[...] = misc
            if feature_major:
                misct_ref[0] = misc.T
            ikd = jnp.where(lane < D_IDX, roped, pltpu.roll(roped, D_IDX, 1))
            s16_ref[:, :LANES] = ikd.astype(BF16)
            s16_ref[:, LANES:] = jnp.zeros((s16_ref.shape[0], W_HEADS - LANES), BF16)
            return
        if g in (G_BQ, G_BK, G_CQ, G_CK, G_IQ):
            val = jnp.concatenate([rope128(h[:, c * LANES:(c + 1) * LANES]) for c in range(W_HEADS // LANES)], axis=1)
        else:
            val = h
        s16_ref[...] = val.astype(BF16)
        if g in r_out:
            r_out[g][...] = val
        if feature_major and (g in t_out or g == G_BV):
            val_t = val.T
            if g in t_out:
                t_out[g][0] = val_t
            if g in (G_AV, G_BV, G_CV):
                vt_ref[0] = val_t.astype(BF16)

    for g in range(N_GROUPS):
        @pl.when(j == g)
        def _(g=g):
            h = h_ref[g & 1]
            if g + 1 < N_GROUPS:
                h_ref[(g + 1) & 1] = project(wn_ref)
            finish(g, h)


def _inproj(x, w_all, layer, cos_t, sin_t, bf_row, seq, feature_major):
    m = x.shape[0]
    tm = min(2 * TM_DENSE, m)
    n_tab = cos_t.shape[0] // tm
    tab_map = lambda i, j: (i % n_tab, 0)
    rows32 = (pl.BlockSpec((tm, W_HEADS), lambda i, j: (i, 0)), jax.ShapeDtypeStruct((m, W_HEADS), F32))
    out_specs = [pl.BlockSpec((tm, W_HEADS), lambda i, j: (i, j)), pl.BlockSpec((tm, LANES), lambda i, j: (i, 0))]
    out_shape = [jax.ShapeDtypeStruct((m, SLAB_W), BF16), jax.ShapeDtypeStruct((m, LANES), F32)]
    if feature_major:
        nt = seq // tm
        bsz = m // seq
        cols32 = (pl.BlockSpec((1, W_HEADS, tm), lambda i, j: (i // nt, 0, i % nt)),
                  jax.ShapeDtypeStruct((bsz, W_HEADS, seq), F32))
        outs = [(pl.BlockSpec((1, W_HEADS, tm), lambda i, j: (i // nt, jnp.minimum(j // 3, 2), i % nt)),
                 jax.ShapeDtypeStruct((bsz, 3 * W_HEADS, seq), BF16)),
                cols32, cols32, cols32, rows32, cols32, cols32,
                (pl.BlockSpec((1, LANES, tm), lambda i, j: (i // nt, 0, i % nt)),
                 jax.ShapeDtypeStruct((bsz, LANES, seq), F32))]
    else:
        outs = [rows32] * 6
    out_specs += [o[0] for o in outs]
    out_shape += [o[1] for o in outs]
    return pl.pallas_call(
        functools.partial(_inproj_kernel, feature_major=feature_major),
        grid=(m // tm, N_GROUPS),
        in_specs=[
            pl.BlockSpec((tm, D_MODEL), lambda i, j: (i, 0)),
            pl.BlockSpec((None, D_MODEL, W_HEADS), lambda i, j: (layer, 0, 0)),
            pl.BlockSpec((None, D_MODEL, W_HEADS), lambda i, j: (layer, 0, jnp.minimum(j + 1, N_GROUPS - 1))),
            pl.BlockSpec((tm, LANES), tab_map),
            pl.BlockSpec((tm, LANES), tab_map),
            pl.BlockSpec((None, 1, LANES), lambda i, j: (layer, 0, 0)),
        ],
        out_specs=out_specs,
        out_shape=out_shape,
        scratch_shapes=[pltpu.VMEM((tm, D_MODEL), BF16), pltpu.VMEM((2, tm, W_HEADS), F32)],
        compiler_params=_cparams(2),
        name="inproj",
    )(x, w_all, w_all, cos_t, sin_t, bf_row)


def _split3(x):
    hi = x.astype(BF16)
    r1 = x - hi.astype(F32)
    mid = r1.astype(BF16)
    lo = (r1 - mid.astype(F32)).astype(BF16)
    return hi, mid, lo


def _cumsum_kernel(x_ref, o_ref, carry_ref):
    t = pl.program_id(1)

    @pl.when(t == 0)
    def _():
        carry_ref[...] = jnp.zeros_like(carry_ref)

    x = x_ref[0]
    tl = x.shape[0]
    r = lax.broadcasted_iota(I32, (tl, tl), 0)
    c = lax.broadcasted_iota(I32, (tl, tl), 1)
    tri = jnp.where(c <= r, 1.0, 0.0).astype(BF16)
    hi, mid, lo = _split3(x)
    dot = lambda a: jnp.dot(tri, a, preferred_element_type=F32)
    cs = (dot(lo) + dot(mid)) + dot(hi) + carry_ref[...]
    o_ref[0] = cs * LOG2E
    carry_ref[...] = cs[tl - 1:tl, :]


def _cumsum_time(x):
    b, l, h = x.shape
    tl = 512 if l % 512 == 0 else l
    return pl.pallas_call(
        _cumsum_kernel,
        grid=(b, l // tl),
        in_specs=[pl.BlockSpec((1, tl, h), lambda i, t: (i, t, 0))],
        out_specs=pl.BlockSpec((1, tl, h), lambda i, t: (i, t, 0)),
        out_shape=jax.ShapeDtypeStruct((b, l, h), F32),
        scratch_shapes=[pltpu.VMEM((1, h), F32)],
        compiler_params=_cparams(2),
        name="cumsum",
    )(x)


def _split_pair(q):
    lane = _lane_iota()
    zero = jnp.zeros_like(q)
    return jnp.where(lane < HEAD_DIM, q, zero), jnp.where(lane >= HEAD_DIM, q, zero)


def _qk(q, k):
    return lax.dot_general(q, k, (((1,), (1,)), ((), ())), preferred_element_type=F32)


def _online_update(s, v, m_old, l_old, acc_old):
    m_new = jnp.maximum(m_old, jnp.max(s, axis=1, keepdims=True))
    m_safe = jnp.where(m_new == NEG_INF, 0.0, m_new)
    alpha = jnp.exp2(m_old - m_safe)
    p = jnp.exp2(s - m_safe)
    l_new = alpha * l_old + jnp.sum(p, axis=1, keepdims=True)
    acc_new = alpha * acc_old + jnp.dot(p.astype(BF16), v, preferred_element_type=F32)
    return m_new, l_new, acc_new


def _causal_mask(kind, rows, cols, row0, col0):
    r = row0 + lax.broadcasted_iota(I32, (rows, 1), 0)
    c = col0 + lax.broadcasted_iota(I32, (1, cols), 1)
    if kind == "fox":
        return c <= r
    return (c >> CHUNK_SHIFT) <= (r >> CHUNK_SHIFT)


def _diff_lambda(lam_ref, lam_init):
    lp = lam_ref[...]
    s1 = jnp.sum(lp[0:1] * lp[1:2], axis=1, keepdims=True)
    s2 = jnp.sum(lp[2:3] * lp[3:4], axis=1, keepdims=True)
    return jnp.exp(s1) - jnp.exp(s2) + lam_init


def _pair_epilogue(kind, o0, o1, lam_ref, g_ref, lam_init):
    if kind == "fox":
        return jnp.where(_lane_iota() < HEAD_DIM, o0, o1)
    o = o0 - _diff_lambda(lam_ref, lam_init) * o1
    o = o * lax.rsqrt(jnp.mean(jnp.square(o), axis=1, keepdims=True) + LN_EPS) * g_ref[...]
    return o * (1.0 - lam_init)


SUM_ROWS = 16


def _with_sum_rows(vt):
    return jnp.concatenate([vt, jnp.ones((SUM_ROWS, vt.shape[1]), vt.dtype)], axis=0)


def _online_update_t(st, vt_aug, m_old, acc_old):
    m_new = jnp.maximum(m_old, jnp.max(st, axis=0, keepdims=True))
    m_safe = jnp.where(m_new == NEG_INF, 0.0, m_new)
    alpha = jnp.exp2(m_old - m_safe)
    pt = jnp.exp2(st - m_safe).astype(BF16)
    acc_new = alpha * acc_old + jnp.dot(vt_aug, pt, preferred_element_type=F32)
    return m_new, acc_new


def _normalised(acc):
    return acc[:LANES] * (1.0 / acc[LANES:LANES + 1])


def _causal_mask_t(kind, keys, queries):
    kp = lax.broadcasted_iota(I32, (keys, 1), 0)
    qp = lax.broadcasted_iota(I32, (1, queries), 1)
    if kind == "fox":
        return kp <= qp
    return (kp >> CHUNK_SHIFT) <= (qp >> CHUNK_SHIFT)


def _init_stats(m_ref, acc_ref):
    m_ref[...] = jnp.full(m_ref.shape, NEG_INF, F32)
    acc_ref[...] = jnp.zeros(acc_ref.shape, F32)


def _pair_attn_prompt_kernel(*refs, kind, tq, lam_init):
    if kind == "fox":
        q_ref, k_ref, vt_ref, cq_ref, ck_ref, o_ref, m_ref, acc_ref, st_ref = refs
        lam_ref = g_ref = None
    else:
        q_ref, k_ref, vt_ref, lam_ref, g_ref, o_ref, m_ref, acc_ref, st_ref = refs
    qi = pl.program_id(2)
    qs = _split_pair(q_ref[...])
    _init_stats(m_ref, acc_ref)

    def scores(j, slot):
        start = pl.multiple_of(j * tq, tq)
        k = k_ref[0, pl.ds(start, tq), :]
        for h in range(2):
            st = _qk(k, qs[h])
            if kind == "fox":
                st = st + cq_ref[0, 0, h:h + 1, :] - ck_ref[0, 0, pl.ds(start, tq), h:h + 1]
            st_ref[slot, h] = st

    def consume(j, slot, masked):
        start = pl.multiple_of(j * tq, tq)
        vt = _with_sum_rows(vt_ref[0, :, pl.ds(start, tq)])
        for h in range(2):
            st = st_ref[slot, h]
            if masked:
                st = jnp.where(_causal_mask_t(kind, tq, tq), st, NEG_INF)
            m_ref[h], acc_ref[h] = _online_update_t(st, vt, m_ref[h], acc_ref[h])

    scores(0, 0)

    def pair(jj, carry):
        j = 2 * jj
        consume(j, 0, False)
        scores(j + 1, 1)
        consume(j + 1, 1, False)
        scores(j + 2, 0)
        return carry

    lax.fori_loop(0, qi // 2, pair, 0)

    @pl.when(qi % 2 == 1)
    def _():
        consume(qi - 1, 0, False)
        scores(qi, 1)
        consume(qi, 1, True)

    @pl.when(qi % 2 == 0)
    def _():
        consume(qi, 0, True)

    o0 = _normalised(acc_ref[0]).T
    o1 = _normalised(acc_ref[1]).T
    o_ref[...] = _pair_epilogue(kind, o0, o1, lam_ref, g_ref, lam_init).astype(BF16)


def _pair_attn_prompt(kind, slab2, slab3, vt_slab, qg, kg, vt_group, extra, lam_init):
    b, t, _ = slab3.shape
    tq = min(TQ_ATTN, t)
    nq = t // tq
    npair = W_HEADS // LANES
    in_specs = [
        pl.BlockSpec((tq, LANES), lambda i, p, q: (i * nq + q, qg * npair + p)),
        pl.BlockSpec((1, t, LANES), lambda i, p, q: (i, 0, kg * npair + p)),
        pl.BlockSpec((1, LANES, t), lambda i, p, q: (i, vt_group * npair + p, 0)),
    ]
    if kind == "fox":
        in_specs += [
            pl.BlockSpec((1, 1, 2, tq), lambda i, p, q: (i, p, 0, q)),
            pl.BlockSpec((1, 1, t, 2), lambda i, p, q: (i, p, 0, 0)),
        ]
    else:
        in_specs += [
            pl.BlockSpec((4, HEAD_DIM), lambda i, p, q: (0, 0)),
            pl.BlockSpec((1, LANES), lambda i, p, q: (0, 0)),
        ]
    return pl.pallas_call(
        functools.partial(_pair_attn_prompt_kernel, kind=kind, tq=tq, lam_init=lam_init),
        grid=(b, npair, nq),
        in_specs=in_specs,
        out_specs=pl.BlockSpec((tq, LANES), lambda i, p, q: (i * nq + q, p)),
        out_shape=jax.ShapeDtypeStruct((b * t, W_HEADS), BF16),
        scratch_shapes=[pltpu.VMEM((2, 1, tq), F32), pltpu.VMEM((2, LANES + SUM_ROWS, tq), F32),
                        pltpu.VMEM((2, 2, tq, tq), F32)],
        compiler_params=_cparams(3),
        name=kind + "_prompt",
    )(slab2, slab3, vt_slab, *extra)


def _pair_attn_sample_kernel(*refs, kind, t, past, lam_init):
    if kind == "fox":
        q_ref, kp_ref, vp_ref, kn_ref, vn_ref, cq_ref, ck_ref, o_ref = refs
        lam_ref = g_ref = None
    else:
        q_ref, kp_ref, vp_ref, kn_ref, vn_ref, lam_ref, g_ref, o_ref = refs
    qs = _split_pair(q_ref[...])
    kp = kp_ref[...].T.astype(BF16)
    vp = (vp_ref[...].T if kind == "fox" else vp_ref[...]).astype(BF16)
    kn = kn_ref[...]
    vn = vn_ref[...]
    outs = []
    for h in range(2):
        m = jnp.full((t, 1), NEG_INF, F32)
        l = jnp.zeros((t, 1), F32)
        acc = jnp.zeros((t, LANES), F32)
        s = _qk(qs[h], kp)
        if kind == "fox":
            s = s + cq_ref[0, 0, :, h:h + 1] - ck_ref[0, 0, h:h + 1, :past]
        m, l, acc = _online_update(s, vp, m, l, acc)
        s = _qk(qs[h], kn)
        if kind == "fox":
            s = s + cq_ref[0, 0, :, h:h + 1] - ck_ref[0, 0, h:h + 1, past:past + t]
        s = jnp.where(_causal_mask(kind, t, t, past, past), s, NEG_INF)
        m, l, acc = _online_update(s, vn, m, l, acc)
        outs.append(acc * (1.0 / l))
    o_ref[...] = _pair_epilogue(kind, outs[0], outs[1], lam_ref, g_ref, lam_init).astype(BF16)


def _pair_attn_sample(kind, slab2, t, cache_k, cache_v, layer, qg, kg, vg, extra, lam_init):
    b = slab2.shape[0] // t
    past = cache_k.shape[3]
    npair = W_HEADS // LANES
    feature_major = pl.BlockSpec((None, None, LANES, past), lambda i, p: (layer, i, p, 0))
    time_major = pl.BlockSpec((None, None, past, LANES), lambda i, p: (layer, i, 0, p))
    in_specs = [
        pl.BlockSpec((t, LANES), lambda i, p: (i, qg * npair + p)),
        feature_major,
        feature_major if kind == "fox" else time_major,
        pl.BlockSpec((t, LANES), lambda i, p: (i, kg * npair + p)),
        pl.BlockSpec((t, LANES), lambda i, p: (i, vg * npair + p)),
    ]
    if kind == "fox":
        in_specs += [
            pl.BlockSpec((1, 1, t, 2), lambda i, p: (i, p, 0, 0)),
            pl.BlockSpec((1, 1, 2, past + t), lambda i, p: (i, p, 0, 0)),
        ]
    else:
        in_specs += [
            pl.BlockSpec((4, HEAD_DIM), lambda i, p: (0, 0)),
            pl.BlockSpec((1, LANES), lambda i, p: (0, 0)),
        ]
    return pl.pallas_call(
        functools.partial(_pair_attn_sample_kernel, kind=kind, t=t, past=past, lam_init=lam_init),
        grid=(b, npair),
        in_specs=in_specs,
        out_specs=pl.BlockSpec((t, LANES), lambda i, p: (i, p)),
        out_shape=jax.ShapeDtypeStruct((b * t, W_HEADS), BF16),
        compiler_params=_cparams(2),
        name=kind + "_sample",
    )(slab2, cache_k, cache_v, slab2, slab2, *extra)


def _dsa_core(iq_ref, iw, ikd_ref, cq_ref, ck_ref, cv_ref, o_ref, keys_ref, bias_ref, m_ref, l_ref, acc_ref,
              *, tq, n_l, ch, qpos0, n_keys, k_sel):
    nch = n_l // ch
    lane = _lane_iota()
    qchunk = (qpos0 + lax.broadcasted_iota(I32, (tq, 1), 0)) >> CHUNK_SHIFT
    w = [iw[:, MISC_IW + h:MISC_IW + h + 1] for h in range(H_IDX)]

    def score_chunk(c, carry):
        start = pl.multiple_of(c * ch, ch)
        ik = ikd_ref[pl.ds(start, ch), :]
        acc = jnp.zeros((tq, ch), F32)
        for hp in range(H_IDX // 2):
            pair = _split_pair(iq_ref[:, hp * LANES:(hp + 1) * LANES])
            for half in range(2):
                acc = acc + jnp.maximum(_qk(pair[half], ik), 0.0) * w[2 * hp + half]
        score = acc * (H_IDX ** -0.5) + 0.0
        bits = lax.bitcast_convert_type(score, I32)
        key = bits ^ ((bits >> 31) & INT_MAX_MASK)
        kpos = start + lax.broadcasted_iota(I32, (1, ch), 1)
        adm = ((kpos >> CHUNK_SHIFT) <= qchunk) & (kpos < n_keys)
        keys_ref[:, pl.ds(start, ch)] = jnp.where(adm, key, INT_MIN)
        return carry

    lax.fori_loop(0, nch, score_chunk, 0)

    def count(pred):
        def body(c, a):
            start = pl.multiple_of(c * ch, ch)
            hit = jnp.where(pred(keys_ref[:, pl.ds(start, ch)]), 1.0, 0.0)
            for t in range(ch // LANES):
                a = a + hit[:, t * LANES:(t + 1) * LANES]
            return a
        part = lax.fori_loop(0, nch, body, jnp.zeros((tq, LANES), F32))
        return jnp.sum(part, axis=1, keepdims=True)

    def bit_step(i, thr_u):
        cand_u = thr_u | lax.shift_left(jnp.int32(1), 31 - i)
        cand_s = cand_u ^ INT_MIN
        n_ge = count(lambda kk: kk >= cand_s)
        return jnp.where(n_ge >= k_sel, cand_u, thr_u)

    thr = lax.fori_loop(0, 32, bit_step, jnp.zeros((tq, 1), I32)) ^ INT_MIN
    need = k_sel - count(lambda kk: kk > thr)

    r128 = lax.broadcasted_iota(I32, (LANES, LANES), 0)
    c128 = lax.broadcasted_iota(I32, (LANES, LANES), 1)
    before = jnp.where(r128 < c128, 1.0, 0.0).astype(BF16)

    def select_block(t, seen):
        start = pl.multiple_of(t * LANES, LANES)
        kk = keys_ref[:, pl.ds(start, LANES)]
        eq = kk == thr
        eqf = jnp.where(eq, 1.0, 0.0)
        rank = seen + jnp.dot(eqf.astype(BF16), before, preferred_element_type=F32)
        sel = ((kk > thr) | (eq & (rank < need))) & (kk != INT_MIN)
        bias_ref[:, pl.ds(start, LANES)] = jnp.where(sel, 0.0, NEG_INF)
        return seen + jnp.sum(eqf, axis=1, keepdims=True)

    lax.fori_loop(0, n_l // LANES, select_block, jnp.zeros((tq, 1), F32))

    for p in range(H_DSA // 2):
        sl = slice(p * LANES, (p + 1) * LANES)
        qs = _split_pair(cq_ref[:, sl])
        m_ref[...] = jnp.full(m_ref.shape, NEG_INF, F32)
        l_ref[...] = jnp.zeros(l_ref.shape, F32)
        acc_ref[...] = jnp.zeros(acc_ref.shape, F32)

        def attend(c, carry):
            start = pl.multiple_of(c * ch, ch)
            k = ck_ref[pl.ds(start, ch), sl]
            v = cv_ref[pl.ds(start, ch), sl]
            bias = bias_ref[:, pl.ds(start, ch)]
            for h in range(2):
                s = _qk(qs[h], k) + bias
                m_ref[h], l_ref[h], acc_ref[h] = _online_update(s, v, m_ref[h], l_ref[h], acc_ref[h])
            return carry

        lax.fori_loop(0, nch, attend, 0)
        o0 = acc_ref[0] * (1.0 / l_ref[0])
        o1 = acc_ref[1] * (1.0 / l_ref[1])
        o_ref[:, sl] = jnp.where(lane < HEAD_DIM, o0, o1).astype(BF16)


def _dsa_scratch(tq, n_l):
    return [pltpu.VMEM((tq, n_l), I32), pltpu.VMEM((tq, n_l), F32),
            pltpu.VMEM((2, tq, 1), F32), pltpu.VMEM((2, tq, 1), F32), pltpu.VMEM((2, tq, LANES), F32)]


def _dsa_core_t(iq_ref, iw_t, ikd_ref, cq_ref, ck_ref, cvt_ref, o_ref, keys_ref, bias_ref, m_ref, acc_ref, st_ref,
                *, tq, n_l, ch, qpos0, n_keys, k_sel):
    nch = n_l // ch
    qchunk = (qpos0 + _lane_iota(tq)) >> CHUNK_SHIFT
    w = [iw_t[MISC_IW + h:MISC_IW + h + 1, :] for h in range(H_IDX)]
    iqs = []
    for hp in range(H_IDX // 2):
        iqs.extend(_split_pair(iq_ref[:, hp * LANES:(hp + 1) * LANES]))

    def score_chunk(c, carry):
        start = pl.multiple_of(c * ch, ch)
        ik = ikd_ref[pl.ds(start, ch), :]
        acc = jnp.zeros((ch, tq), F32)
        for h in range(H_IDX):
            acc = acc + jnp.maximum(_qk(ik, iqs[h]), 0.0) * w[h]
        score = acc * (H_IDX ** -0.5) + 0.0
        bits = lax.bitcast_convert_type(score, I32)
        key = bits ^ ((bits >> 31) & INT_MAX_MASK)
        kpos = start + lax.broadcasted_iota(I32, (ch, 1), 0)
        adm = ((kpos >> CHUNK_SHIFT) <= qchunk) & (kpos < n_keys)
        keys_ref[pl.ds(start, ch), :] = jnp.where(adm, key, INT_MIN)
        return carry

    lax.fori_loop(0, nch, score_chunk, 0)

    n_acc = 4

    def count(pred):
        def body(c, accs):
            start = pl.multiple_of(c * ch, ch)
            kk = keys_ref[pl.ds(start, ch), :]
            accs = list(accs)
            for g in range(ch // 8):
                a = accs[g % n_acc]
                accs[g % n_acc] = jnp.where(pred(kk[g * 8:(g + 1) * 8, :]), a + 1.0, a)
            return tuple(accs)
        accs = lax.fori_loop(0, nch, body, tuple(jnp.zeros((8, tq), F32) for _ in range(n_acc)))
        return jnp.sum((accs[0] + accs[1]) + (accs[2] + accs[3]), axis=0, keepdims=True)

    def bit_step(i, thr_u):
        cand_u = thr_u | lax.shift_left(jnp.int32(1), 31 - i)
        cand_s = cand_u ^ INT_MIN
        n_ge = count(lambda kk: kk >= cand_s)
        return jnp.where(n_ge >= k_sel, cand_u, thr_u)

    thr = lax.fori_loop(0, 32, bit_step, jnp.zeros((1, tq), I32)) ^ INT_MIN
    need = k_sel - count(lambda kk: kk > thr)

    sb = 2 * LANES
    r_i = lax.broadcasted_iota(I32, (sb, sb), 0)
    c_i = lax.broadcasted_iota(I32, (sb, sb), 1)
    before = jnp.where(c_i < r_i, 1.0, 0.0).astype(BF16)

    def select_block(t, seen):
        start = pl.multiple_of(t * sb, sb)
        kk = keys_ref[pl.ds(start, sb), :]
        eq = kk == thr
        eqf = jnp.where(eq, 1.0, 0.0)
        rank = seen + jnp.dot(before, eqf.astype(BF16), preferred_element_type=F32)
        sel = ((kk > thr) | (eq & (rank < need))) & (kk != INT_MIN)
        bias_ref[pl.ds(start, sb), :] = jnp.where(sel, 0.0, NEG_INF)
        return seen + jnp.sum(eqf, axis=0, keepdims=True)

    def select_all_equal(c, carry):
        start = pl.multiple_of(c * ch, ch)
        kk = keys_ref[pl.ds(start, ch), :]
        bias_ref[pl.ds(start, ch), :] = jnp.where((kk >= thr) & (kk != INT_MIN), 0.0, NEG_INF)
        return carry

    n_eq = count(lambda kk: kk == thr)
    partial_ties = jnp.max(jnp.where(n_eq > need, 1, 0)) > 0

    @pl.when(partial_ties)
    def _():
        lax.fori_loop(0, n_l // sb, select_block, jnp.zeros((1, tq), F32))

    @pl.when(jnp.logical_not(partial_ties))
    def _():
        lax.fori_loop(0, nch, select_all_equal, 0)

    head0_rows = lax.broadcasted_iota(I32, (LANES, 1), 0) < HEAD_DIM
    for p in range(H_DSA // 2):
        sl = slice(p * LANES, (p + 1) * LANES)
        qs = _split_pair(cq_ref[:, sl])
        _init_stats(m_ref, acc_ref)

        def scores(c, slot):
            start = pl.multiple_of(c * ch, ch)
            k = ck_ref[pl.ds(start, ch), sl]
            bias = bias_ref[pl.ds(start, ch), :]
            for h in range(2):
                st_ref[slot, h] = _qk(k, qs[h]) + bias

        def consume(c, slot):
            start = pl.multiple_of(c * ch, ch)
            vt = _with_sum_rows(cvt_ref[sl, pl.ds(start, ch)])
            for h in range(2):
                m_ref[h], acc_ref[h] = _online_update_t(st_ref[slot, h], vt, m_ref[h], acc_ref[h])

        scores(0, 0)

        def attend_pair(cc, carry):
            c = 2 * cc
            consume(c, 0)
            scores(c + 1, 1)
            consume(c + 1, 1)
            scores(c + 2, 0)
            return carry

        lax.fori_loop(0, (nch - 1) // 2, attend_pair, 0)
        if nch % 2 == 0:
            consume(nch - 2, 0)
            scores(nch - 1, 1)
            consume(nch - 1, 1)
        else:
            consume(nch - 1, 0)
        ot = jnp.where(head0_rows, _normalised(acc_ref[0]), _normalised(acc_ref[1]))
        o_ref[:, sl] = ot.T.astype(BF16)


def _dsa_prompt_kernel(iq_ref, misc_ref, ikd_ref, cq_ref, ck_ref, cvt_ref, o_ref, *scratch, tq, n_l, q0, n_keys, k_sel):
    qpos0 = q0 + pl.program_id(1) * tq
    _dsa_core_t(iq_ref, misc_ref[...].T, ikd_ref.at[0], cq_ref, ck_ref.at[0], cvt_ref.at[0], o_ref, *scratch,
                tq=tq, n_l=n_l, ch=min(CH_DSA, n_l), qpos0=qpos0, n_keys=n_keys, k_sel=k_sel)


def _dsa_prompt_segment(slab2, slab3, vt_slab, misc, seg, seg_len, k_sel):
    b, t, _ = slab3.shape
    tq = min(TQ_DSA, seg_len)
    nqs = seg_len // tq
    nq = t // tq
    n_l = (seg + 1) * seg_len
    row = lambda i, q: i * nq + seg * nqs + q
    npair = W_HEADS // LANES
    return pl.pallas_call(
        functools.partial(_dsa_prompt_kernel, tq=tq, n_l=n_l, q0=seg * seg_len, n_keys=t, k_sel=k_sel),
        grid=(b, nqs),
        in_specs=[
            pl.BlockSpec((tq, W_HEADS), lambda i, q: (row(i, q), G_IQ)),
            pl.BlockSpec((tq, LANES), lambda i, q: (row(i, q), 0)),
            pl.BlockSpec((1, n_l, LANES), lambda i, q: (i, 0, G_MISC * npair)),
            pl.BlockSpec((tq, W_HEADS), lambda i, q: (row(i, q), G_CQ)),
            pl.BlockSpec((1, n_l, W_HEADS), lambda i, q: (i, 0, G_CK)),
            pl.BlockSpec((1, W_HEADS, n_l), lambda i, q: (i, 2, 0)),
        ],
        out_specs=pl.BlockSpec((tq, W_HEADS), lambda i, q: (i * nqs + q, 0)),
        out_shape=jax.ShapeDtypeStruct((b * seg_len, W_HEADS), BF16),
        scratch_shapes=[pltpu.VMEM((n_l, tq), I32), pltpu.VMEM((n_l, tq), F32),
                        pltpu.VMEM((2, 1, tq), F32), pltpu.VMEM((2, LANES + SUM_ROWS, tq), F32),
                        pltpu.VMEM((2, 2, min(CH_DSA, n_l), tq), F32)],
        compiler_params=_cparams(2),
        name="dsa_prompt",
    )(slab2, misc, slab3, slab2, slab3, vt_slab)


def _dsa_prompt(slab2, slab3, vt_slab, misc, k_sel):
    b, t, _ = slab3.shape
    seg_len = min(SEG_DSA, t)
    segs = [_dsa_prompt_segment(slab2, slab3, vt_slab, misc, s, seg_len, k_sel) for s in range(t // seg_len)]
    y = jnp.stack([s.reshape(b, seg_len, W_HEADS) for s in segs], axis=1)
    return y.reshape(b * t, W_HEADS)


def _dsa_sample_kernel(iq_ref, misc_ref, ikp_ref, ckp_ref, cvp_ref, ikn_ref, cq_ref, ckn_ref, cvn_ref, o_ref,
                       ik_s, ck_s, cv_s, *scratch, t, past, n_l, ch, k_sel):
    pad = n_l - past - t
    ikp = ikp_ref[...]
    ik_s[:past, :] = jnp.concatenate([ikp, ikp], axis=0).T.astype(BF16)
    ik_s[past:past + t, :] = ikn_ref[...]
    ik_s[past + t:, :] = jnp.zeros((pad, LANES), BF16)
    ck_s[:past, :] = ckp_ref[...].T.astype(BF16)
    ck_s[past:past + t, :] = ckn_ref[...]
    ck_s[past + t:, :] = jnp.zeros((pad, W_HEADS), BF16)
    cv_s[:past, :] = cvp_ref[...].T.astype(BF16)
    cv_s[past:past + t, :] = cvn_ref[...]
    cv_s[past + t:, :] = jnp.zeros((pad, W_HEADS), BF16)
    _dsa_core(iq_ref, misc_ref[...], ik_s, cq_ref, ck_s, cv_s, o_ref, *scratch,
              tq=t, n_l=n_l, ch=ch, qpos0=past, n_keys=past + t, k_sel=k_sel)


def _dsa_sample(slab2, misc, t, ikd_past, cache_k, cache_v, layer, k_sel):
    b = slab2.shape[0] // t
    past = cache_k.shape[3]
    n_l = -(-(past + t) // LANES) * LANES
    ch = n_l
    npair = W_HEADS // LANES
    return pl.pallas_call(
        functools.partial(_dsa_sample_kernel, t=t, past=past, n_l=n_l, ch=ch, k_sel=k_sel),
        grid=(b,),
        in_specs=[
            pl.BlockSpec((t, W_HEADS), lambda i: (i, G_IQ)),
            pl.BlockSpec((t, LANES), lambda i: (i, 0)),
            pl.BlockSpec((None, None, D_IDX, past), lambda i: (layer, i, 0, 0)),
            pl.BlockSpec((None, None, W_HEADS, past), lambda i: (layer, i, 0, 0)),
            pl.BlockSpec((None, None, W_HEADS, past), lambda i: (layer, i, 0, 0)),
            pl.BlockSpec((t, LANES), lambda i: (i, G_MISC * npair)),
            pl.BlockSpec((t, W_HEADS), lambda i: (i, G_CQ)),
            pl.BlockSpec((t, W_HEADS), lambda i: (i, G_CK)),
            pl.BlockSpec((t, W_HEADS), lambda i: (i, G_CV)),
        ],
        out_specs=pl.BlockSpec((t, W_HEADS), lambda i: (i, 0)),
        out_shape=jax.ShapeDtypeStruct((b * t, W_HEADS), BF16),
        scratch_shapes=[pltpu.VMEM((n_l, LANES), BF16), pltpu.VMEM((n_l, W_HEADS), BF16),
                        pltpu.VMEM((n_l, W_HEADS), BF16)] + _dsa_scratch(t, n_l),
        compiler_params=_cparams(1),
        name="dsa_sample",
    )(slab2, misc, ikd_past, cache_k, cache_v, slab2, slab2, slab2, slab2)


def _layernorm(z, g_ref, b_ref):
    mu = jnp.mean(z, axis=1, keepdims=True)
    d = z - mu
    var = jnp.mean(jnp.square(d), axis=1, keepdims=True)
    return d * lax.rsqrt(var + LN_EPS) * g_ref[...] + b_ref[...]


def _merge_kernel(x_ref, ya_ref, yb_ref, yc_ref, wgl_ref, bg_ref, wa_ref, wb_ref, wc_ref, wo_ref, g_ref, b_ref, o_ref):
    x = x_ref[...]
    xb = x.astype(BF16)
    merged = None
    for i, (y_ref, w_ref) in enumerate(((ya_ref, wa_ref), (yb_ref, wb_ref), (yc_ref, wc_ref))):
        gl = jnp.dot(xb, wgl_ref[:, i * D_MODEL:(i + 1) * D_MODEL], preferred_element_type=F32)
        gate = jax.nn.sigmoid(gl + bg_ref[i:i + 1, :])
        term = gate * jnp.dot(y_ref[...], w_ref[...], preferred_element_type=F32)
        merged = term if merged is None else merged + term
    z = DEEPNORM_ALPHA * x + jnp.dot(merged.astype(BF16), wo_ref[...], preferred_element_type=F32)
    o_ref[...] = _layernorm(z, g_ref, b_ref)


def _resident(shape, layer):
    nd = len(shape)
    return pl.BlockSpec((None,) + tuple(shape), lambda i: (layer,) + (0,) * nd, pipeline_mode=pl.Buffered(1))


def _merge(x, ya, yb, yc, wts, layer):
    m = x.shape[0]
    tm = min(TM_DENSE, m)
    row = lambda w: pl.BlockSpec((tm, w), lambda i: (i, 0))
    return pl.pallas_call(
        _merge_kernel,
        grid=(m // tm,),
        in_specs=[
            row(D_MODEL), row(W_HEADS), row(W_HEADS), row(W_HEADS),
            _resident((D_MODEL, N_BRANCH * D_MODEL), layer),
            _resident((N_BRANCH, D_MODEL), layer),
            _resident((W_HEADS, D_MODEL), layer),
            _resident((W_HEADS, D_MODEL), layer),
            _resident((W_HEADS, D_MODEL), layer),
            _resident((D_MODEL, D_MODEL), layer),
            _resident((1, D_MODEL), layer),
            _resident((1, D_MODEL), layer),
        ],
        out_specs=row(D_MODEL),
        out_shape=jax.ShapeDtypeStruct((m, D_MODEL), F32),
        compiler_params=_cparams(1),
        name="merge",
    )(x, ya, yb, yc, wts["w_gl"], wts["b_gate"], wts["w_br_a"], wts["w_br_b"], wts["w_br_c"], wts["w_o"],
      wts["ln1_g"], wts["ln1_b"])


def _ffn_kernel(x_ref, w1_ref, w3_ref, w2_ref, g_ref, b_ref, o_ref, *, d_ff, fc):
    x = x_ref[...]
    xb = x.astype(BF16)
    acc = None
    for c in range(d_ff // fc):
        sl = slice(c * fc, (c + 1) * fc)
        h1 = jnp.dot(xb, w1_ref[:, sl], preferred_element_type=F32)
        h3 = jnp.dot(xb, w3_ref[:, sl], preferred_element_type=F32)
        u = (h1 * jax.nn.sigmoid(h1) * h3).astype(BF16)
        term = jnp.dot(u, w2_ref[sl, :], preferred_element_type=F32)
        acc = term if acc is None else acc + term
    o_ref[...] = _layernorm(DEEPNORM_ALPHA * x + acc, g_ref, b_ref)


def _ffn(x, wts, layer):
    m = x.shape[0]
    tm = min(TM_DENSE, m)
    d_ff = wts["w_ff1"].shape[2]
    row = pl.BlockSpec((tm, D_MODEL), lambda i: (i, 0))
    return pl.pallas_call(
        functools.partial(_ffn_kernel, d_ff=d_ff, fc=256),
        grid=(m // tm,),
        in_specs=[
            row,
            _resident((D_MODEL, d_ff), layer),
            _resident((D_MODEL, d_ff), layer),
            _resident((d_ff, D_MODEL), layer),
            _resident((1, D_MODEL), layer),
            _resident((1, D_MODEL), layer),
        ],
        out_specs=row,
        out_shape=jax.ShapeDtypeStruct((m, D_MODEL), F32),
        compiler_params=_cparams(1),
        name="ffn",
    )(x, wts["w_ff1"], wts["w_ff3"], wts["w_ff2"], wts["ln2_g"], wts["ln2_b"])


def _rope_tables(pos, tm):
    half = HEAD_DIM // 2
    inv = ROPE_THETA ** (-jnp.arange(half, dtype=F32) / half)
    ang = pos.astype(F32)[:, None] * inv[None, :]
    cos, sin = jnp.cos(ang), jnp.sin(ang)
    cos_t = jnp.concatenate([cos, cos, cos, cos], axis=1)
    sin_t = jnp.concatenate([-sin, sin, -sin, sin], axis=1)
    reps = max(1, tm // pos.shape[0])
    return jnp.tile(cos_t, (reps, 1)), jnp.tile(sin_t, (reps, 1))


def _prep_weights(w_in, b_fgate, b_gate, diff_norm_g, w_br_a, w_br_b, w_br_c, w_o, ln1_g, ln1_b, ln2_g, ln2_b,
                  w_ff1, w_ff3, w_ff2, lam_q1, lam_k1, lam_q2, lam_k2):
    depth = w_in.shape[0]
    offs = np.cumsum((0,) + SPLIT_SIZES)
    col = lambda i: w_in[:, :, offs[i]:offs[i + 1]]
    aq, ak, av, af, bq, bk, bv, cq, ck, cv, iq, ik, iw, gl = (col(i) for i in range(len(SPLIT_SIZES)))
    qs = HEAD_DIM ** -0.5 * LOG2E
    misc_pad = jnp.zeros((depth, D_MODEL, W_HEADS - D_IDX - H_FOX - H_IDX), w_in.dtype)
    w_slab = jnp.concatenate([aq * qs, ak, av, bq * qs, bk, bv, cq * qs, ck, cv, iq * (D_IDX ** -0.5),
                              ik, af, iw, misc_pad], axis=2).astype(BF16)
    bf_row = jnp.zeros((depth, 1, LANES), F32).at[:, 0, MISC_LOGF:MISC_LOGF + H_FOX].set(b_fgate.astype(F32))
    return {
        "w_slab": w_slab,
        "bf_row": bf_row,
        "w_gl": gl.astype(BF16),
        "b_gate": b_gate.astype(F32),
        "w_br_a": w_br_a.astype(BF16), "w_br_b": w_br_b.astype(BF16), "w_br_c": w_br_c.astype(BF16),
        "w_o": w_o.astype(BF16),
        "ln1_g": ln1_g[:, None, :].astype(F32), "ln1_b": ln1_b[:, None, :].astype(F32),
        "ln2_g": ln2_g[:, None, :].astype(F32), "ln2_b": ln2_b[:, None, :].astype(F32),
        "w_ff1": w_ff1.astype(BF16), "w_ff3": w_ff3.astype(BF16), "w_ff2": w_ff2.astype(BF16),
        "lam4": jnp.stack([lam_q1, lam_k1, lam_q2, lam_k2], axis=1).astype(F32),
        "diff_g": diff_norm_g[:, None, :].astype(F32),
    }


def _layer_group(x, wts, layer, tabs, bsz, t, caches):
    cos_t, sin_t = tabs
    proj = _inproj(x, wts["w_slab"], layer, cos_t, sin_t, wts["bf_row"], t, caches is None)
    slab, misc = proj[:2]
    logf = misc[:, MISC_LOGF:MISC_LOGF + H_FOX].reshape(bsz, t, H_FOX)
    lf_all = misc.reshape(bsz, t, LANES)
    past = 0
    if caches is not None:
        past = caches["a_logf"].shape[2]
        past_lf = jnp.pad(caches["a_logf"][layer].astype(F32),
                          ((0, 0), (0, 0), (MISC_LOGF, LANES - MISC_LOGF - H_FOX)))
        lf_all = jnp.concatenate([past_lf, lf_all], axis=1)
    n_keys = past + t
    cum = _cumsum_time(lf_all)[:, :, MISC_LOGF:MISC_LOGF + H_FOX]
    npair = H_FOX // 2
    cum_rows = cum.transpose(0, 2, 1).reshape(bsz, npair, 2, n_keys)
    cum_cols = cum.reshape(bsz, n_keys, npair, 2).transpose(0, 2, 1, 3)
    lam_init = 0.8 - 0.6 * math.exp(-0.3 * layer)
    diff_extra = (wts["lam4"][layer], wts["diff_g"][layer])
    k_sel = min(TOPK_MAX, n_keys // 4)
    if caches is None:
        slab3 = slab.reshape(bsz, t, SLAB_W)
        vt_slab = proj[2]
        ya = _pair_attn_prompt("fox", slab, slab3, vt_slab, G_AQ, G_AK, 0, (cum_rows, cum_cols), lam_init)
        yb = _pair_attn_prompt("diff", slab, slab3, vt_slab, G_BQ, G_BK, 1, diff_extra, lam_init)
        yc = _dsa_prompt(slab, slab3, vt_slab, misc, k_sel)
    else:
        cq4, ck4 = cum_cols[:, :, past:], cum_rows
        ya = _pair_attn_sample("fox", slab, t, caches["a_k"], caches["a_v"], layer, G_AQ, G_AK, G_AV,
                               (cq4, ck4), lam_init)
        yb = _pair_attn_sample("diff", slab, t, caches["b_k"], caches["b_v"], layer, G_BQ, G_BK, G_BV,
                               diff_extra, lam_init)
        yc = _dsa_sample(slab, misc, t, caches["c_idx"], caches["c_k"], caches["c_v"], layer, k_sel)
    x = _merge(x, ya, yb, yc, wts, layer)
    x = _ffn(x, wts, layer)
    if caches is None:
        akt, avt, bkt, bv, ckt, cvt, misct = proj[3:]
        new_rows = (akt, avt, misct[:, MISC_LOGF:MISC_LOGF + H_FOX], bkt, bv, ckt, cvt, misct[:, :D_IDX])
    else:
        ak, av, bk, bv, ck, cv = proj[2:]
        new_rows = (
            ak.reshape(bsz, t, H_FOX, HEAD_DIM), av.reshape(bsz, t, H_FOX, HEAD_DIM), logf,
            bk.reshape(bsz, t, H_DIFF, 2, HEAD_DIM), bv.reshape(bsz, t, H_DIFF, 2 * HEAD_DIM),
            ck.reshape(bsz, t, H_DSA, HEAD_DIM), cv.reshape(bsz, t, H_DSA, HEAD_DIM),
            misc[:, :D_IDX].reshape(bsz, t, D_IDX),
        )
    return x, new_rows


def _prompt_outputs(rows, bsz, t):
    depth = len(rows)
    st = lambda i: jnp.stack([r[i] for r in rows])
    heads = lambda a, *dims: jnp.moveaxis(a.reshape((depth, bsz) + dims + (t,)), -1, 2)
    return [
        heads(st(0), H_FOX, HEAD_DIM), heads(st(1), H_FOX, HEAD_DIM), heads(st(2), H_FOX),
        heads(st(3), H_DIFF, 2, HEAD_DIM), st(4).reshape(depth, bsz, t, H_DIFF, 2 * HEAD_DIM),
        heads(st(5), H_DSA, HEAD_DIM), heads(st(6), H_DSA, HEAD_DIM), heads(st(7), D_IDX),
    ]


def kernel(x_prompt, x_sample, cache_a_k, cache_a_v, cache_a_logf, cache_b_k, cache_b_v, cache_c_k, cache_c_v, cache_c_idx, w_in, b_fgate, b_gate, lam_q1, lam_k1, lam_q2, lam_k2, diff_norm_g, w_br_a, w_br_b, w_br_c, w_o, ln1_g, ln1_b, ln2_g, ln2_b, w_ff1, w_ff3, w_ff2):
    depth = w_in.shape[0]
    bp, tp, _ = x_prompt.shape
    bs, ts, _ = x_sample.shape
    past = cache_a_k.shape[2]
    wts = _prep_weights(w_in, b_fgate, b_gate, diff_norm_g, w_br_a, w_br_b, w_br_c, w_o, ln1_g, ln1_b, ln2_g, ln2_b,
                        w_ff1, w_ff3, w_ff2, lam_q1, lam_k1, lam_q2, lam_k2)
    tabs_p = _rope_tables(jnp.arange(tp, dtype=I32), min(2 * TM_DENSE, bp * tp))
    tabs_s = _rope_tables(past + jnp.arange(ts, dtype=I32), min(2 * TM_DENSE, bs * ts))
    flat = lambda c: c.reshape(c.shape[0], c.shape[1], c.shape[2], -1)
    fmaj = lambda c: jnp.moveaxis(flat(c), 2, 3)
    caches = {
        "a_k": fmaj(cache_a_k), "a_v": fmaj(cache_a_v), "a_logf": cache_a_logf,
        "b_k": fmaj(cache_b_k), "b_v": flat(cache_b_v),
        "c_k": fmaj(cache_c_k), "c_v": fmaj(cache_c_v), "c_idx": fmaj(cache_c_idx),
    }
    yp = x_prompt.reshape(bp * tp, D_MODEL)
    ys = x_sample.reshape(bs * ts, D_MODEL)
    rows_p, rows_s = [], []
    for layer in range(depth):
        yp, rp = _layer_group(yp, wts, layer, tabs_p, bp, tp, None)
        ys, rs = _layer_group(ys, wts, layer, tabs_s, bs, ts, caches)
        rows_p.append(rp)
        rows_s.append(rs)
    outs_p = _prompt_outputs(rows_p, bp, tp)
    outs_s = [jnp.stack([r[i] for r in rows_s]) for i in range(8)]
    return (yp.reshape(bp, tp, D_MODEL), ys.reshape(bs, ts, D_MODEL), *outs_p, *outs_s)
```
